```python
import jax, jax.numpy as jnp
from jax import lax
import numpy as np

D_MODEL = 4096
BATCH = 4
SEQ = 2048
DEPTH = 2
DEC_BATCH = 128
DEC_SEQ = 4
PAST_LEN = 16384
PAGE_SIZE = 128

D_MIX = D_MODEL
D_A = D_MIX // 4
D_B = D_MIX // 2
D_C = D_MIX - D_A - D_B
DK = 128
DV = 128
H_B = D_B // DV
H_C = 8
HEAD_C = D_C // H_C
CONV_A_W = 31
CONV_B_W = 4
CHUNK_B = 64
CHUNK_C = 128
D_FF = 11008
EPS = 1e-6
N_IN = 2 * D_A + 4 * D_B + 2 * H_B + 2 * D_C
SPLITS = [2 * D_A, 2 * D_A + 3 * D_B, 2 * D_A + 4 * D_B, 2 * D_A + 4 * D_B + H_B, 2 * D_A + 4 * D_B + 2 * H_B]

kernel_name = 'hybrid_conv_deltanet_chunkmlp_macaron_step'


def _rmsnorm(x, g):
    xf = x.astype(jnp.float32)
    y = xf * lax.rsqrt(jnp.mean(xf * xf, axis=-1, keepdims=True) + EPS)
    return (y * g.astype(jnp.float32)).astype(x.dtype)


def _layernorm(x, g, b):
    xf = x.astype(jnp.float32)
    mu = jnp.mean(xf, axis=-1, keepdims=True)
    var = jnp.mean(jnp.square(xf - mu), axis=-1, keepdims=True)
    y = (xf - mu) * lax.rsqrt(var + EPS) * g.astype(jnp.float32) + b.astype(jnp.float32)
    return y.astype(x.dtype)


def _l2norm(x):
    xf = x.astype(jnp.float32)
    return xf * lax.rsqrt(jnp.sum(xf * xf, axis=-1, keepdims=True) + EPS)


def _swiglu(x, w_gate, w_up, w_down):
    return (jax.nn.silu(x @ w_gate) * (x @ w_up)) @ w_down


def _causal_depthwise(x_ext, w):
    c = x_ext.shape[-1]
    return lax.conv_general_dilated(
        x_ext, w.astype(x_ext.dtype)[:, None, :], window_strides=(1,), padding='VALID',
        dimension_numbers=('NWC', 'WIO', 'NWC'), feature_group_count=c)


def _gated_delta(q, k, v, g, beta, s0):
    B, L = q.shape[0], q.shape[1]
    c = min(CHUNK_B, L)
    n = -(-L // c)
    pad = n * c - L
    f32 = jnp.float32

    def prep(t):
        t = t.astype(f32)
        if pad:
            t = jnp.pad(t, [(0, 0), (0, pad)] + [(0, 0)] * (t.ndim - 2))
        t = t.reshape((B, n, c) + t.shape[2:])
        return jnp.moveaxis(t, 3, 1)

    qc, kc, vc, bc = prep(q), prep(k), prep(v), prep(beta)
    gc = jnp.cumsum(prep(g), axis=-1)
    tri_incl = jnp.tril(jnp.ones((c, c), dtype=bool))
    tri_strict = jnp.tril(jnp.ones((c, c), dtype=bool), -1)
    diff = gc[..., :, None] - gc[..., None, :]
    decay = jnp.where(tri_incl, jnp.exp(jnp.where(tri_incl, diff, 0.0)), 0.0)
    gam = jnp.exp(gc)
    kk = jnp.einsum('bhnid,bhnjd->bhnij', kc, kc)
    lmat = jnp.where(tri_strict, bc[..., :, None] * kk * decay, 0.0) + jnp.eye(c, dtype=f32)
    rhs = jnp.concatenate([bc[..., None] * vc, (bc * gam)[..., None] * kc], axis=-1)
    sol = lax.linalg.triangular_solve(lmat, rhs, left_side=True, lower=True, unit_diagonal=True)
    u_base, w_mat = sol[..., :DV], sol[..., DV:]
    a_qk = jnp.einsum('bhnid,bhnjd->bhnij', qc, kc) * decay
    q_dec = qc * gam[..., None]
    k_dec = kc * jnp.exp(gc[..., -1:] - gc)[..., None]
    g_last = gam[..., -1]

    def step(s, xs):
        u_n, w_n, aqk_n, qd_n, kd_n, gl_n = xs
        u = u_n - jnp.einsum('bhck,bhkv->bhcv', w_n, s)
        o = jnp.einsum('bhck,bhkv->bhcv', qd_n, s) + jnp.einsum('bhij,bhjv->bhiv', aqk_n, u)
        s = gl_n[..., None, None] * s + jnp.einsum('bhck,bhcv->bhkv', kd_n, u)
        return s, o

    xs = tuple(jnp.moveaxis(t, 2, 0) for t in (u_base, w_mat, a_qk, q_dec, k_dec, g_last))
    s_fin, o = lax.scan(step, s0.astype(f32), xs)
    o = jnp.transpose(o, (1, 0, 3, 2, 4)).reshape(B, n * c, H_B, DV)[:, :L]
    return o, s_fin


def _chunk_spatial_gate(u, v, w_s, b_s):
    B, L = u.shape[0], u.shape[1]
    c = CHUNK_C if L >= CHUNK_C else L
    n = L // c
    ws = jnp.where(jnp.tril(jnp.ones((c, c), dtype=bool)), w_s[:, :c, :c], 0.0).astype(v.dtype)
    vc = v.reshape(B, n, c, H_C, HEAD_C)
    mixed = jnp.einsum('hij,bnjhd->bnihd', ws, vc) + b_s[:, :c].T[None, None, :, :, None]
    return u * mixed.reshape(B, L, H_C, HEAD_C)


def _token_mix(h, buf_a, buf_b, s0, lp):
    B, L, _ = h.shape
    p = h @ lp['w_in']
    pa, pqkv, pz, pdec, pbeta, puv = jnp.split(p, SPLITS, axis=-1)
    ga = pa[..., :D_A] * jax.nn.sigmoid(pa[..., D_A:])
    ext_a = jnp.concatenate([buf_a.astype(ga.dtype), ga], axis=1)
    ya = _causal_depthwise(ext_a, lp['conv_a_w']) + lp['conv_a_b']
    ya = jax.nn.silu(_layernorm(ya, lp['norm_a_g'], lp['norm_a_b']))
    new_a = ext_a[:, -(CONV_A_W - 1):]
    ext_b = jnp.concatenate([buf_b.astype(pqkv.dtype), pqkv], axis=1)
    qkv = jax.nn.silu(_causal_depthwise(ext_b, lp['conv_b_w']))
    new_b = ext_b[:, -(CONV_B_W - 1):]
    q, k, v = jnp.split(qkv, 3, axis=-1)
    q = _l2norm(q.reshape(B, L, H_B, DK)) * (DK ** -0.5)
    k = _l2norm(k.reshape(B, L, H_B, DK))
    v = v.reshape(B, L, H_B, DV)
    beta = jax.nn.sigmoid(pbeta.astype(jnp.float32))
    g = -jnp.exp(lp['a_log'].astype(jnp.float32)) * jax.nn.softplus(pdec.astype(jnp.float32) + lp['dt_bias'].astype(jnp.float32))
    o, s_new = _gated_delta(q, k, v, g, beta, s0)
    yb = _rmsnorm(o, lp['norm_o_g']).astype(h.dtype) * jax.nn.silu(pz.reshape(B, L, H_B, DV))
    yb = yb.reshape(B, L, D_B)
    uv = jax.nn.gelu(puv, approximate=False)
    u, vv = jnp.split(uv, 2, axis=-1)
    vv = _layernorm(vv, lp['norm_c_g'], lp['norm_c_b'])
    yc = _chunk_spatial_gate(u.reshape(B, L, H_C, HEAD_C), vv.reshape(B, L, H_C, HEAD_C), lp['sgu_w'], lp['sgu_b'])
    yc = yc.reshape(B, L, D_C)
    c = CHUNK_C if L >= CHUNK_C else L
    new_v = vv[:, L - c:]
    y = jnp.concatenate([ya, yb.astype(ya.dtype), yc], axis=-1) @ lp['w_out']
    return y, new_a, new_b, s_new, new_v


def _trunk(x, conv_a, conv_b, delta, prm, norm_final):
    new_a, new_b, new_s, new_v = [], [], [], []
    for i in range(DEPTH):
        lp = {name: arr[i] for name, arr in prm.items()}
        x = x + 0.5 * _swiglu(_rmsnorm(x, lp['norm_ffn1']), lp['ffn1_gate'], lp['ffn1_up'], lp['ffn1_down'])
        y, a_i, b_i, s_i, v_i = _token_mix(_rmsnorm(x, lp['norm_mix']), conv_a[:, i], conv_b[:, i], delta[:, i], lp)
        x = x + y
        x = x + 0.5 * _swiglu(_rmsnorm(x, lp['norm_ffn2']), lp['ffn2_gate'], lp['ffn2_up'], lp['ffn2_down'])
        new_a.append(a_i); new_b.append(b_i); new_s.append(s_i); new_v.append(v_i)
    y = _rmsnorm(x, norm_final)
    return y, jnp.stack(new_a, 1), jnp.stack(new_b, 1), jnp.stack(new_s, 1), jnp.stack(new_v, 1)


def setup_inputs(seed: int = 0) -> dict:
    key = jax.random.key(seed)
    ks = iter(jax.random.split(key, 40))

    def nrm(shape, s):
        return jax.random.normal(next(ks), shape, jnp.float32) * s

    def gain(shape):
        return 1.0 + nrm(shape, 0.02)

    dt = jax.random.uniform(next(ks), (DEPTH, H_B), jnp.float32, minval=1e-3, maxval=1e-1)
    return {
        'x_prompt': nrm((BATCH, SEQ, D_MODEL), 1.0),
        'x_sample': nrm((DEC_BATCH, DEC_SEQ, D_MODEL), 1.0),
        'state_conv_a': nrm((DEC_BATCH, DEPTH, CONV_A_W - 1, D_A), 0.5),
        'state_conv_b': nrm((DEC_BATCH, DEPTH, CONV_B_W - 1, 3 * D_B), 1.0),
        'state_delta': nrm((DEC_BATCH, DEPTH, H_B, DK, DV), 0.5),
        'norm_ffn1': gain((DEPTH, D_MODEL)),
        'ffn1_gate': nrm((DEPTH, D_MODEL, D_FF), D_MODEL ** -0.5),
        'ffn1_up': nrm((DEPTH, D_MODEL, D_FF), D_MODEL ** -0.5),
        'ffn1_down': nrm((DEPTH, D_FF, D_MODEL), D_FF ** -0.5),
        'norm_mix': gain((DEPTH, D_MODEL)),
        'w_in': nrm((DEPTH, D_MODEL, N_IN), D_MODEL ** -0.5),
        'conv_a_w': nrm((DEPTH, CONV_A_W, D_A), CONV_A_W ** -0.5),
        'conv_a_b': nrm((DEPTH, D_A), 0.02),
        'norm_a_g': gain((DEPTH, D_A)),
        'norm_a_b': nrm((DEPTH, D_A), 0.02),
        'conv_b_w': nrm((DEPTH, CONV_B_W, 3 * D_B), CONV_B_W ** -0.5),
        'a_log': jnp.log(jax.random.uniform(next(ks), (DEPTH, H_B), jnp.float32, minval=1.0, maxval=16.0)),
        'dt_bias': dt + jnp.log(-jnp.expm1(-dt)),
        'norm_o_g': gain((DEPTH, DV)),
        'norm_c_g': gain((DEPTH, D_C)),
        'norm_c_b': nrm((DEPTH, D_C), 0.02),
        'sgu_w': nrm((DEPTH, H_C, CHUNK_C, CHUNK_C), CHUNK_C ** -0.5),
        'sgu_b': 1.0 + nrm((DEPTH, H_C, CHUNK_C), 0.02),
        'w_out': nrm((DEPTH, D_MIX, D_MODEL), D_MIX ** -0.5),
        'norm_ffn2': gain((DEPTH, D_MODEL)),
        'ffn2_gate': nrm((DEPTH, D_MODEL, D_FF), D_MODEL ** -0.5),
        'ffn2_up': nrm((DEPTH, D_MODEL, D_FF), D_MODEL ** -0.5),
        'ffn2_down': nrm((DEPTH, D_FF, D_MODEL), D_FF ** -0.5),
        'norm_final': gain((D_MODEL,)),
    }


def reference(x_prompt, x_sample, state_conv_a, state_conv_b, state_delta,
              norm_ffn1, ffn1_gate, ffn1_up, ffn1_down, norm_mix, w_in,
              conv_a_w, conv_a_b, norm_a_g, norm_a_b, conv_b_w, a_log, dt_bias, norm_o_g,
              norm_c_g, norm_c_b, sgu_w, sgu_b, w_out,
              norm_ffn2, ffn2_gate, ffn2_up, ffn2_down, norm_final):
    prm = dict(norm_ffn1=norm_ffn1, ffn1_gate=ffn1_gate, ffn1_up=ffn1_up, ffn1_down=ffn1_down,
               norm_mix=norm_mix, w_in=w_in, conv_a_w=conv_a_w, conv_a_b=conv_a_b,
               norm_a_g=norm_a_g, norm_a_b=norm_a_b, conv_b_w=conv_b_w, a_log=a_log,
               dt_bias=dt_bias, norm_o_g=norm_o_g, norm_c_g=norm_c_g, norm_c_b=norm_c_b,
               sgu_w=sgu_w, sgu_b=sgu_b, w_out=w_out, norm_ffn2=norm_ffn2,
               ffn2_gate=ffn2_gate, ffn2_up=ffn2_up, ffn2_down=ffn2_down)
    zero_a = jnp.zeros((BATCH, DEPTH, CONV_A_W - 1, D_A), x_prompt.dtype)
    zero_b = jnp.zeros((BATCH, DEPTH, CONV_B_W - 1, 3 * D_B), x_prompt.dtype)
    zero_s = jnp.zeros((BATCH, DEPTH, H_B, DK, DV), jnp.float32)
    y_prompt, conv_a_prompt, conv_b_prompt, delta_prompt, chunk_v_prompt = _trunk(
        x_prompt, zero_a, zero_b, zero_s, prm, norm_final)
    y_sample, conv_a_sample, conv_b_sample, delta_sample, chunk_v_sample = _trunk(
        x_sample, state_conv_a, state_conv_b, state_delta, prm, norm_final)
    return (y_prompt, y_sample, conv_a_prompt, conv_b_prompt, delta_prompt, chunk_v_prompt,
            conv_a_sample, conv_b_sample, delta_sample, chunk_v_sample)
```

```python
import functools

import numpy as np
import jax
import jax.numpy as jnp
from jax import lax
from jax.experimental import pallas as pl
from jax.experimental.pallas import tpu as pltpu

F32 = jnp.float32
BF16 = jnp.bfloat16
HIGHEST = lax.Precision.HIGHEST

D_MODEL = 4096
BATCH = 4
SEQ = 2048
DEPTH = 2
DEC_BATCH = 128
DEC_SEQ = 4
D_A = 1024
D_B = 2048
D_C = 1024
DK = 128
DV = 128
H_B = 16
H_C = 8
HEAD_C = 128
CONV_A_W = 31
CONV_B_W = 4
CHUNK_C = 128
D_FF = 11008
EPS = 1e-6

T_PROMPT = BATCH * SEQ
T_SAMPLE = DEC_BATCH * DEC_SEQ
T_ALL = T_PROMPT + T_SAMPLE

N_PROJ = 12800
COL_BLK = 2048
SMALL_BLK = 512
LANES = 128
SUBLANES = 8

VMEM_LIMIT = 52 * 1024 * 1024

TM = 512
TF = 256
TN_IN = 512
TN_OUT = 512
TL_A = 256
TL_C = 256
CH_B = 128
SB = 8


def _cparams(sem):
    return pltpu.CompilerParams(dimension_semantics=sem, vmem_limit_bytes=VMEM_LIMIT)


def _sigmoid(x):
    return jax.nn.sigmoid(x)


def _silu(x):
    return x * _sigmoid(x)


def _rms(x, g):
    return x * lax.rsqrt(jnp.mean(x * x, axis=-1, keepdims=True) + EPS) * g


def _layernorm(x, g, b):
    mu = jnp.mean(x, axis=-1, keepdims=True)
    xc = x - mu
    var = jnp.mean(xc * xc, axis=-1, keepdims=True)
    return xc * lax.rsqrt(var + EPS) * g + b


def _dot(a, b, precision=None):
    return jnp.dot(a, b, preferred_element_type=F32, precision=precision)


def _dot_nt(a, b, precision=None):
    return lax.dot_general(a, b, (((1,), (1,)), ((), ())), preferred_element_type=F32,
                           precision=precision)


def _dot_tn(a, b, precision=None):
    return lax.dot_general(a, b, (((0,), (0,)), ((), ())), preferred_element_type=F32,
                           precision=precision)


def _ffn_body(x_ref, g_ref, wg_ref, wu_ref, wd_ref, gf_ref, o_ref, h_ref, *, final_norm):
    j = pl.program_id(1)

    @pl.when(j == 0)
    def _():
        x = x_ref[...]
        h_ref[...] = _rms(x, g_ref[...]).astype(BF16)
        o_ref[...] = x

    h = h_ref[...]
    hg = _dot(h, wg_ref[...])
    hu = _dot(h, wu_ref[...])
    a = (0.5 * _silu(hg) * hu).astype(BF16)
    o_ref[...] += _dot(a, wd_ref[...])

    if final_norm:
        @pl.when(j == pl.num_programs(1) - 1)
        def _():
            o_ref[...] = _rms(o_ref[...], gf_ref[...])


def _ffn(x, g, wg, wu, wd, gf, final_norm):
    t = x.shape[0]
    nf = wg.shape[1] // TF
    return pl.pallas_call(
        functools.partial(_ffn_body, final_norm=final_norm),
        grid=(t // TM, nf),
        in_specs=[
            pl.BlockSpec((TM, D_MODEL), lambda i, j: (i, 0), pipeline_mode=pl.Buffered(1)),
            pl.BlockSpec((1, D_MODEL), lambda i, j: (0, 0)),
            pl.BlockSpec((D_MODEL, TF), lambda i, j: (0, j)),
            pl.BlockSpec((D_MODEL, TF), lambda i, j: (0, j)),
            pl.BlockSpec((TF, D_MODEL), lambda i, j: (j, 0)),
            pl.BlockSpec((1, D_MODEL), lambda i, j: (0, 0)),
        ],
        out_specs=pl.BlockSpec((TM, D_MODEL), lambda i, j: (i, 0)),
        out_shape=jax.ShapeDtypeStruct((t, D_MODEL), F32),
        scratch_shapes=[pltpu.VMEM((TM, D_MODEL), BF16)],
        compiler_params=_cparams(("parallel", "arbitrary")),
        name="ffn_final" if final_norm else "ffn",
    )(x, g, wg, wu, wd, gf)


def _proj_in_body(x_ref, g_ref, w_ref, o_ref, h_ref):
    @pl.when(pl.program_id(1) == 0)
    def _():
        h_ref[...] = _rms(x_ref[...], g_ref[...]).astype(BF16)

    o_ref[...] = _dot(h_ref[...], w_ref[...])


def _proj_in(x, g, w):
    t = x.shape[0]
    return pl.pallas_call(
        _proj_in_body,
        grid=(t // TM, N_PROJ // TN_IN),
        in_specs=[
            pl.BlockSpec((TM, D_MODEL), lambda i, j: (i, 0)),
            pl.BlockSpec((1, D_MODEL), lambda i, j: (0, 0)),
            pl.BlockSpec((D_MODEL, TN_IN), lambda i, j: (0, j)),
        ],
        out_specs=pl.BlockSpec((TM, TN_IN), lambda i, j: (i, j)),
        out_shape=jax.ShapeDtypeStruct((t, N_PROJ), F32),
        scratch_shapes=[pltpu.VMEM((TM, D_MODEL), BF16)],
        compiler_params=_cparams(("parallel", "arbitrary")),
        name="proj_in",
    )(x, g, w)


def _proj_out_body(x_ref, ya_ref, yb_ref, yc_ref, w_ref, o_ref):
    acc = x_ref[...]
    acc += _dot(ya_ref[...], w_ref[0:D_A, :])
    acc += _dot(yb_ref[...], w_ref[D_A:D_A + D_B, :])
    acc += _dot(yc_ref[...], w_ref[D_A + D_B:, :])
    o_ref[...] = acc


def _proj_out(x, ya, yb, yc, w):
    t = x.shape[0]
    return pl.pallas_call(
        _proj_out_body,
        grid=(t // TM, D_MODEL // TN_OUT),
        in_specs=[
            pl.BlockSpec((TM, TN_OUT), lambda i, j: (i, j)),
            pl.BlockSpec((TM, D_A), lambda i, j: (i, 0)),
            pl.BlockSpec((TM, D_B), lambda i, j: (i, 0)),
            pl.BlockSpec((TM, D_C), lambda i, j: (i, 0)),
            pl.BlockSpec((D_MODEL, TN_OUT), lambda i, j: (0, j)),
        ],
        out_specs=pl.BlockSpec((TM, TN_OUT), lambda i, j: (i, j)),
        out_shape=jax.ShapeDtypeStruct((t, D_MODEL), F32),
        compiler_params=_cparams(("parallel", "arbitrary")),
        name="proj_out",
    )(x, ya, yb, yc, w)


HIST = 32


def _conv_a_prompt_body(pa_ref, w_ref, b_ref, lg_ref, lb_ref, ya_ref, st_ref, ext_ref):
    l = pl.program_id(1)

    @pl.when(l == 0)
    def _():
        ext_ref[0:HIST, :] = jnp.zeros((HIST, D_A), F32)

    pa = pa_ref[...]
    ext_ref[HIST:HIST + TL_A, :] = pa[:, :D_A] * _sigmoid(pa[:, D_A:])
    off = HIST - (CONV_A_W - 1)
    acc = jnp.broadcast_to(b_ref[...], (TL_A, D_A))
    for w in range(CONV_A_W):
        acc = acc + ext_ref[off + w:off + w + TL_A, :] * w_ref[w:w + 1, :]
    ya_ref[...] = _silu(_layernorm(acc, lg_ref[...], lb_ref[...])).astype(BF16)

    @pl.when(l == pl.num_programs(1) - 1)
    def _():
        st_ref[0] = ext_ref[HIST + TL_A - (CONV_A_W - 1):HIST + TL_A, :]

    ext_ref[0:HIST, :] = ext_ref[TL_A:TL_A + HIST, :]


def _conv_a_prompt(p, w, b, lg, lb):
    nl = SEQ // TL_A
    return pl.pallas_call(
        _conv_a_prompt_body,
        grid=(BATCH, nl),
        in_specs=[
            pl.BlockSpec((TL_A, COL_BLK), lambda bi, l: (bi * nl + l, 0)),
            pl.BlockSpec((CONV_A_W, D_A), lambda bi, l: (0, 0)),
            pl.BlockSpec((1, D_A), lambda bi, l: (0, 0)),
            pl.BlockSpec((1, D_A), lambda bi, l: (0, 0)),
            pl.BlockSpec((1, D_A), lambda bi, l: (0, 0)),
        ],
        out_specs=[
            pl.BlockSpec((TL_A, D_A), lambda bi, l: (bi * nl + l, 0)),
            pl.BlockSpec((1, CONV_A_W - 1, D_A), lambda bi, l: (bi, 0, 0)),
        ],
        out_shape=[
            jax.ShapeDtypeStruct((T_PROMPT, D_A), BF16),
            jax.ShapeDtypeStruct((BATCH, CONV_A_W - 1, D_A), F32),
        ],
        scratch_shapes=[pltpu.VMEM((HIST + TL_A, D_A), F32)],
        compiler_params=_cparams(("parallel", "arbitrary")),
        name="conv_a_prompt",
    )(p, w, b, lg, lb)


def _conv_a_sample_body(pa_ref, st_ref, w_ref, b_ref, lg_ref, lb_ref, ya_ref, ns_ref):
    ns = CONV_A_W - 1
    ga = []
    for t in range(DEC_SEQ):
        pa = pa_ref[t]
        ga.append(pa[:, :D_A] * _sigmoid(pa[:, D_A:]))

    def ext_row(r):
        return st_ref[r] if r < ns else ga[r - ns]

    for t in range(DEC_SEQ):
        acc = jnp.broadcast_to(b_ref[...], (SB, D_A))
        for w in range(CONV_A_W):
            acc = acc + ext_row(t + w) * w_ref[w:w + 1, :]
        ya_ref[t] = _silu(_layernorm(acc, lg_ref[...], lb_ref[...])).astype(BF16)
    for r in range(ns):
        ns_ref[r] = ext_row(r + DEC_SEQ)


def _conv_a_sample(ps, state_t, w, b, lg, lb):
    ns = CONV_A_W - 1
    return pl.pallas_call(
        _conv_a_sample_body,
        grid=(DEC_BATCH // SB,),
        in_specs=[
            pl.BlockSpec((DEC_SEQ, SB, COL_BLK), lambda s: (0, s, 0)),
            pl.BlockSpec((ns, SB, D_A), lambda s: (0, s, 0)),
            pl.BlockSpec((CONV_A_W, D_A), lambda s: (0, 0)),
            pl.BlockSpec((1, D_A), lambda s: (0, 0)),
            pl.BlockSpec((1, D_A), lambda s: (0, 0)),
            pl.BlockSpec((1, D_A), lambda s: (0, 0)),
        ],
        out_specs=[
            pl.BlockSpec((DEC_SEQ, SB, D_A), lambda s: (0, s, 0)),
            pl.BlockSpec((ns, SB, D_A), lambda s: (0, s, 0)),
        ],
        out_shape=[
            jax.ShapeDtypeStruct((DEC_SEQ, DEC_BATCH, D_A), BF16),
            jax.ShapeDtypeStruct((ns, DEC_BATCH, D_A), F32),
        ],
        compiler_params=_cparams(("parallel",)),
        name="conv_a_sample",
    )(ps, state_t, w, b, lg, lb)


_SQRT_HALF = float(np.sqrt(0.5))


def _gelu(x):
    return 0.5 * x * (1.0 + lax.erf(x * _SQRT_HALF))


def _gmlp_prompt_body(puv_ref, ng_ref, nb_ref, ws_ref, bst_ref, yc_ref, cv_ref):
    l = pl.program_id(1)
    uv = _gelu(puv_ref[...])
    u = uv[:, :D_C]
    vv = _layernorm(uv[:, D_C:], ng_ref[...], nb_ref[...])
    row = lax.broadcasted_iota(jnp.int32, (CHUNK_C, CHUNK_C), 0)
    col = lax.broadcasted_iota(jnp.int32, (CHUNK_C, CHUNK_C), 1)
    vb = vv.astype(BF16)
    for h in range(H_C):
        ws = jnp.where(row >= col, ws_ref[h], 0.0).astype(BF16)
        hs = slice(h * HEAD_C, (h + 1) * HEAD_C)
        for c in range(TL_C // CHUNK_C):
            rs = slice(c * CHUNK_C, (c + 1) * CHUNK_C)
            mixed = _dot(ws, vb[rs, hs]) + bst_ref[:, h:h + 1]
            yc_ref[rs, hs] = (u[rs, hs] * mixed).astype(BF16)

    @pl.when(l == pl.num_programs(1) - 1)
    def _():
        cv_ref[0] = vv[TL_C - CHUNK_C:, :]


def _gmlp_prompt(p, ng, nb, ws, bst):
    nl = SEQ // TL_C
    return pl.pallas_call(
        _gmlp_prompt_body,
        grid=(BATCH, nl),
        in_specs=[
            pl.BlockSpec((TL_C, COL_BLK), lambda bi, l: (bi * nl + l, 5)),
            pl.BlockSpec((1, D_C), lambda bi, l: (0, 0)),
            pl.BlockSpec((1, D_C), lambda bi, l: (0, 0)),
            pl.BlockSpec((H_C, CHUNK_C, CHUNK_C), lambda bi, l: (0, 0, 0)),
            pl.BlockSpec((CHUNK_C, H_C), lambda bi, l: (0, 0)),
        ],
        out_specs=[
            pl.BlockSpec((TL_C, D_C), lambda bi, l: (bi * nl + l, 0)),
            pl.BlockSpec((1, CHUNK_C, D_C), lambda bi, l: (bi, 0, 0)),
        ],
        out_shape=[
            jax.ShapeDtypeStruct((T_PROMPT, D_C), BF16),
            jax.ShapeDtypeStruct((BATCH, CHUNK_C, D_C), F32),
        ],
        compiler_params=_cparams(("parallel", "arbitrary")),
        name="gmlp_prompt",
    )(p, ng, nb, ws, bst)


def _gmlp_sample_body(puv_ref, ng_ref, nb_ref, wsm_ref, bsm_ref, yc_ref, cv_ref):
    u, vv = [], []
    for t in range(DEC_SEQ):
        uv = _gelu(puv_ref[t])
        u.append(uv[:, :D_C])
        vt = _layernorm(uv[:, D_C:], ng_ref[...], nb_ref[...])
        vv.append(vt)
        cv_ref[t] = vt
    for i in range(DEC_SEQ):
        mixed = jnp.broadcast_to(bsm_ref[i:i + 1, :], (SB, D_C))
        for j in range(i + 1):
            mixed = mixed + wsm_ref[i * DEC_SEQ + j:i * DEC_SEQ + j + 1, :] * vv[j]
        yc_ref[i] = (u[i] * mixed).astype(BF16)


def _gmlp_sample(ps, ng, nb, wsm, bsm):
    return pl.pallas_call(
        _gmlp_sample_body,
        grid=(DEC_BATCH // SB,),
        in_specs=[
            pl.BlockSpec((DEC_SEQ, SB, COL_BLK), lambda s: (0, s, 5)),
            pl.BlockSpec((1, D_C), lambda s: (0, 0)),
            pl.BlockSpec((1, D_C), lambda s: (0, 0)),
            pl.BlockSpec((DEC_SEQ * DEC_SEQ, D_C), lambda s: (0, 0)),
            pl.BlockSpec((SUBLANES, D_C), lambda s: (0, 0)),
        ],
        out_specs=[
            pl.BlockSpec((DEC_SEQ, SB, D_C), lambda s: (0, s, 0)),
            pl.BlockSpec((DEC_SEQ, SB, D_C), lambda s: (0, s, 0)),
        ],
        out_shape=[
            jax.ShapeDtypeStruct((DEC_SEQ, DEC_BATCH, D_C), BF16),
            jax.ShapeDtypeStruct((DEC_SEQ, DEC_BATCH, D_C), F32),
        ],
        compiler_params=_cparams(("parallel",)),
        name="gmlp_sample",
    )(ps, ng, nb, wsm, bsm)


def _softplus(x):
    return jnp.maximum(x, 0.0) + jnp.log1p(jnp.exp(-jnp.abs(x)))


def _l2norm(x):
    return x * lax.rsqrt(jnp.sum(x * x, axis=-1, keepdims=True) + EPS)


def _unit_lower_inverse(lmat, row, col):
    n = lmat.shape[0]
    eye = jnp.where(row == col, 1.0, 0.0).astype(F32)
    neg = jnp.where((row >> 3) == (col >> 3), -lmat, 0.0)
    t = eye + neg
    p2 = _dot(neg, neg, HIGHEST)
    t = t + _dot(t, p2, HIGHEST)
    p4 = _dot(p2, p2, HIGHEST)
    t = t + _dot(t, p4, HIGHEST)
    sh = 3
    while (1 << sh) < n:
        off = jnp.where(((row >> (sh + 1)) == (col >> (sh + 1))) & ((row >> sh) != (col >> sh)),
                        lmat, 0.0)
        t = t - _dot(_dot(t, off, HIGHEST), t, HIGHEST)
        sh += 1
    return t


def _delta_prompt_body(q_ref, k_ref, v_ref, z_ref, db_ref, cw_ref, alog_ref, dtb_ref, og_ref,
                       yb_ref, cb_ref, so_ref,
                       ext_ref, s_ref, qs_ref, ks_ref, vs_ref, gc_ref, gr_ref, bc_ref, os_ref):
    c = pl.program_id(1)
    n = CH_B
    hist = SUBLANES

    @pl.when(c == 0)
    def _():
        ext_ref[0:hist, :] = jnp.zeros((hist, 3 * D_B), F32)
        s_ref[...] = jnp.zeros((H_B, DK, DV), F32)

    ext_ref[hist:hist + n, 0:D_B] = q_ref[...]
    ext_ref[hist:hist + n, D_B:2 * D_B] = k_ref[...]
    ext_ref[hist:hist + n, 2 * D_B:] = v_ref[...]

    @pl.when(c == pl.num_programs(1) - 1)
    def _():
        cb_ref[0] = ext_ref[hist + n - (CONV_B_W - 1):hist + n, :]

    off = hist - (CONV_B_W - 1)
    acc = ext_ref[off:off + n, :] * cw_ref[0:1, :]
    for w in range(1, CONV_B_W):
        acc = acc + ext_ref[off + w:off + w + n, :] * cw_ref[w:w + 1, :]
    qkv = _silu(acc)
    ext_ref[0:hist, :] = ext_ref[n:n + hist, :]

    row = lax.broadcasted_iota(jnp.int32, (n, n), 0)
    col = lax.broadcasted_iota(jnp.int32, (n, n), 1)
    incl = row >= col

    db = db_ref[:, 0:LANES]
    g = -jnp.exp(alog_ref[...]) * _softplus(db + dtb_ref[...])
    beta = _sigmoid(db)
    gcs = _dot(jnp.where(incl, 1.0, 0.0).astype(F32), g, HIGHEST)
    gcs_t = gcs.T
    for h in range(H_B):
        hs = slice(h * DK, (h + 1) * DK)
        qs_ref[h] = _l2norm(qkv[:, hs]) * (DK ** -0.5)
        ks_ref[h] = _l2norm(qkv[:, D_B + h * DK:D_B + (h + 1) * DK])
        vs_ref[h] = qkv[:, 2 * D_B + h * DV:2 * D_B + (h + 1) * DV]
        gc_ref[h] = jnp.broadcast_to(gcs[:, h:h + 1], (n, n))
        gr_ref[h] = jnp.broadcast_to(gcs_t[h:h + 1, :], (n, n))
        bc_ref[h] = jnp.broadcast_to(beta[:, H_B + h:H_B + h + 1], (n, n))

    def head(h, carry):
        q = qs_ref[h]
        k = ks_ref[h]
        v = vs_ref[h]
        gcol = gc_ref[h]
        bcol = bc_ref[h]
        decay = jnp.where(incl, jnp.exp(jnp.where(incl, gcol - gr_ref[h], 0.0)), 0.0)
        gam = jnp.exp(gcol)
        glast = gcol[n - 1:n, :]
        kb = k.astype(BF16)
        qb = q.astype(BF16)
        lmat = jnp.where(row > col, bcol * _dot_nt(kb, kb) * decay, 0.0)
        tinv = _unit_lower_inverse(lmat, row, col)
        u0 = _dot(tinv, bcol * v, HIGHEST)
        wm = _dot(tinv, (bcol * gam) * k, HIGHEST)
        aqk = _dot_nt(qb, kb) * decay
        s = s_ref[h]
        sb = s.astype(BF16)
        u = u0 - _dot(wm.astype(BF16), sb)
        ub = u.astype(BF16)
        o = _dot((q * gam).astype(BF16), sb) + _dot(aqk.astype(BF16), ub)
        kd = k * jnp.exp(glast - gcol)
        s_ref[h] = jnp.exp(glast) * s + _dot_tn(kd.astype(BF16), ub)
        os_ref[h] = _rms(o, og_ref[...])
        return carry

    lax.fori_loop(0, H_B, head, 0)

    for h in range(H_B):
        hs = slice(h * DV, (h + 1) * DV)
        yb_ref[:, hs] = (os_ref[h] * _silu(z_ref[:, hs])).astype(BF16)

    @pl.when(c == pl.num_programs(1) - 1)
    def _():
        so_ref[0] = s_ref[...]


def _delta_prompt(p, cw, alog, dtb, og):
    nc = SEQ // CH_B
    pad_col = (N_PROJ - SMALL_BLK) // SMALL_BLK
    tile = pltpu.VMEM((H_B, CH_B, CH_B), F32)
    return pl.pallas_call(
        _delta_prompt_body,
        grid=(BATCH, nc),
        in_specs=[
            pl.BlockSpec((CH_B, COL_BLK), lambda bi, c: (bi * nc + c, 1)),
            pl.BlockSpec((CH_B, COL_BLK), lambda bi, c: (bi * nc + c, 2)),
            pl.BlockSpec((CH_B, COL_BLK), lambda bi, c: (bi * nc + c, 3)),
            pl.BlockSpec((CH_B, COL_BLK), lambda bi, c: (bi * nc + c, 4)),
            pl.BlockSpec((CH_B, SMALL_BLK), lambda bi, c: (bi * nc + c, pad_col)),
            pl.BlockSpec((CONV_B_W, 3 * D_B), lambda bi, c: (0, 0)),
            pl.BlockSpec((1, LANES), lambda bi, c: (0, 0)),
            pl.BlockSpec((1, LANES), lambda bi, c: (0, 0)),
            pl.BlockSpec((1, DV), lambda bi, c: (0, 0)),
        ],
        out_specs=[
            pl.BlockSpec((CH_B, D_B), lambda bi, c: (bi * nc + c, 0)),
            pl.BlockSpec((1, CONV_B_W - 1, 3 * D_B), lambda bi, c: (bi, 0, 0)),
            pl.BlockSpec((1, H_B, DK, DV), lambda bi, c: (bi, 0, 0, 0)),
        ],
        out_shape=[
            jax.ShapeDtypeStruct((T_PROMPT, D_B), BF16),
            jax.ShapeDtypeStruct((BATCH, CONV_B_W - 1, 3 * D_B), F32),
            jax.ShapeDtypeStruct((BATCH, H_B, DK, DV), F32),
        ],
        scratch_shapes=[
            pltpu.VMEM((SUBLANES + CH_B, 3 * D_B), F32),
            pltpu.VMEM((H_B, DK, DV), F32),
            tile, tile, tile, tile, tile, tile, tile,
        ],
        compiler_params=_cparams(("parallel", "arbitrary")),
        name="delta_prompt",
    )(p, p, p, p, p, cw, alog, dtb, og)


def _delta_sample_body(q_ref, k_ref, v_ref, z_ref, db_ref, st_ref, s0_ref, cw_ref, alog_ref,
                       dtb_ref, og_ref, yb_ref, cb_ref, so_ref):
    ns = CONV_B_W - 1
    nt = DEC_SEQ

    def ext_part(r, lo, ref):
        return st_ref[r, :, lo:lo + D_B] if r < ns else ref[r - ns]

    def conv(lo, ref):
        out = []
        for t in range(nt):
            acc = ext_part(t, lo, ref) * cw_ref[0:1, lo:lo + D_B]
            for w in range(1, CONV_B_W):
                acc = acc + ext_part(t + w, lo, ref) * cw_ref[w:w + 1, lo:lo + D_B]
            out.append(_silu(acc))
        return out

    qc, kc, vc = conv(0, q_ref), conv(D_B, k_ref), conv(2 * D_B, v_ref)
    for r in range(ns):
        for lo, ref in ((0, q_ref), (D_B, k_ref), (2 * D_B, v_ref)):
            cb_ref[r, :, lo:lo + D_B] = ext_part(r + nt, lo, ref)

    g, beta = [], []
    for t in range(nt):
        db = db_ref[t][:, 0:LANES]
        g.append(-jnp.exp(alog_ref[...]) * _softplus(db + dtb_ref[...]))
        beta.append(_sigmoid(db))
    gc = [g[0]]
    for t in range(1, nt):
        gc.append(gc[t - 1] + g[t])

    rowid = lax.broadcasted_iota(jnp.int32, (SB, DV), 0)
    rows4 = lax.broadcasted_iota(jnp.int32, (nt * SB, DV), 0) & (SB - 1)

    for h in range(H_B):
        hs = slice(h * DK, (h + 1) * DK)
        q = [_l2norm(qc[t][:, hs]) * (DK ** -0.5) for t in range(nt)]
        k = [_l2norm(kc[t][:, hs]) for t in range(nt)]
        v = [vc[t][:, hs] for t in range(nt)]
        gch = [gc[t][:, h:h + 1] for t in range(nt)]
        bh = [beta[t][:, H_B + h:H_B + h + 1] for t in range(nt)]
        gam = [jnp.exp(x) for x in gch]

        lhs = jnp.concatenate(k + [q[t] * gam[t] for t in range(nt)], axis=0).astype(BF16)
        ks0 = [jnp.zeros((SB, DV), F32) for _ in range(nt)]
        qs0 = [jnp.zeros((SB, DV), F32) for _ in range(nt)]
        for s in range(SB):
            res = _dot(lhs, s0_ref[s, h].astype(BF16))
            for t in range(nt):
                ks0[t] = jnp.where(rowid == s, res[t * SB:(t + 1) * SB], ks0[t])
                qs0[t] = jnp.where(rowid == s, res[(nt + t) * SB:(nt + t + 1) * SB], qs0[t])

        u = []
        for t in range(nt):
            acc = bh[t] * (v[t] - gam[t] * ks0[t])
            for j in range(t):
                kk = jnp.sum(k[t] * k[j], axis=-1, keepdims=True)
                acc = acc - (bh[t] * kk * jnp.exp(gch[t] - gch[j])) * u[j]
            u.append(acc)

        for t in range(nt):
            o = qs0[t]
            for j in range(t + 1):
                qk = jnp.sum(q[t] * k[j], axis=-1, keepdims=True)
                o = o + (qk * jnp.exp(gch[t] - gch[j])) * u[j]
            yb_ref[t, :, hs] = (_rms(o, og_ref[...]) * _silu(z_ref[t][:, hs])).astype(BF16)

        kd = jnp.concatenate([k[t] * jnp.exp(gch[nt - 1] - gch[t]) for t in range(nt)],
                             axis=0).astype(BF16)
        uall = jnp.concatenate(u, axis=0)
        for s in range(SB):
            um = jnp.where(rows4 == s, uall, 0.0).astype(BF16)
            gl = jnp.exp(gch[nt - 1][s:s + 1, :])
            so_ref[s, h] = gl * s0_ref[s, h] + _dot_tn(kd, um)


def _delta_sample(ps, conv_state_t, delta_state, layer, cw, alog, dtb, og):
    ns = CONV_B_W - 1
    pad_col = (N_PROJ - SMALL_BLK) // SMALL_BLK
    return pl.pallas_call(
        _delta_sample_body,
        grid=(DEC_BATCH // SB,),
        in_specs=[
            pl.BlockSpec((DEC_SEQ, SB, COL_BLK), lambda s: (0, s, 1)),
            pl.BlockSpec((DEC_SEQ, SB, COL_BLK), lambda s: (0, s, 2)),
            pl.BlockSpec((DEC_SEQ, SB, COL_BLK), lambda s: (0, s, 3)),
            pl.BlockSpec((DEC_SEQ, SB, COL_BLK), lambda s: (0, s, 4)),
            pl.BlockSpec((DEC_SEQ, SB, SMALL_BLK), lambda s: (0, s, pad_col)),
            pl.BlockSpec((ns, SB, 3 * D_B), lambda s: (0, s, 0)),
            pl.BlockSpec((SB, None, H_B, DK, DV), lambda s: (s, layer, 0, 0, 0)),
            pl.BlockSpec((CONV_B_W, 3 * D_B), lambda s: (0, 0)),
            pl.BlockSpec((1, LANES), lambda s: (0, 0)),
            pl.BlockSpec((1, LANES), lambda s: (0, 0)),
            pl.BlockSpec((1, DV), lambda s: (0, 0)),
        ],
        out_specs=[
            pl.BlockSpec((DEC_SEQ, SB, D_B), lambda s: (0, s, 0)),
            pl.BlockSpec((ns, SB, 3 * D_B), lambda s: (0, s, 0)),
            pl.BlockSpec((SB, H_B, DK, DV), lambda s: (s, 0, 0, 0)),
        ],
        out_shape=[
            jax.ShapeDtypeStruct((DEC_SEQ, DEC_BATCH, D_B), BF16),
            jax.ShapeDtypeStruct((ns, DEC_BATCH, 3 * D_B), F32),
            jax.ShapeDtypeStruct((DEC_BATCH, H_B, DK, DV), F32),
        ],
        compiler_params=_cparams(("parallel",)),
        name="delta_sample",
    )(ps, ps, ps, ps, ps, conv_state_t, delta_state, cw, alog, dtb, og)


def _prep_w_in(w_in):
    a0, a1 = 0, 2 * D_A
    q1 = a1 + 3 * D_B
    z1 = q1 + D_B
    d1 = z1 + 2 * H_B
    pad = jnp.zeros(w_in.shape[:2] + (SMALL_BLK - 2 * H_B,), w_in.dtype)
    cols = [w_in[..., a0:a1], w_in[..., a1:q1], w_in[..., q1:z1], w_in[..., d1:],
            w_in[..., z1:d1], pad]
    return jnp.concatenate(cols, axis=-1).astype(BF16)


def _row(x):
    return x.reshape(1, -1)


def _pad_lanes(x):
    return jnp.pad(x, (0, LANES - x.shape[0])).reshape(1, LANES)


def kernel(x_prompt, x_sample, state_conv_a, state_conv_b, state_delta, norm_ffn1, ffn1_gate,
           ffn1_up, ffn1_down, norm_mix, w_in, conv_a_w, conv_a_b, norm_a_g, norm_a_b, conv_b_w,
           a_log, dt_bias, norm_o_g, norm_c_g, norm_c_b, sgu_w, sgu_b, w_out, norm_ffn2,
           ffn2_gate, ffn2_up, ffn2_down, norm_final):
    wg1, wu1, wd1 = ffn1_gate.astype(BF16), ffn1_up.astype(BF16), ffn1_down.astype(BF16)
    wg2, wu2, wd2 = ffn2_gate.astype(BF16), ffn2_up.astype(BF16), ffn2_down.astype(BF16)
    w_in_b = _prep_w_in(w_in)
    w_out_b = w_out.astype(BF16)

    xs = jnp.transpose(x_sample, (1, 0, 2)).reshape(T_SAMPLE, D_MODEL)
    x = jnp.concatenate([x_prompt.reshape(T_PROMPT, D_MODEL), xs], axis=0)
    gfin = _row(norm_final)

    wsm = jnp.repeat(sgu_w[:, :, :DEC_SEQ, :DEC_SEQ].reshape(DEPTH, H_C, DEC_SEQ * DEC_SEQ),
                     HEAD_C, axis=1)
    wsm = jnp.transpose(wsm, (0, 2, 1))
    bsm = jnp.transpose(jnp.repeat(sgu_b[:, :, :SUBLANES], HEAD_C, axis=1), (0, 2, 1))
    bst = jnp.transpose(sgu_b, (0, 2, 1))

    outs = {k: [] for k in ("ap", "bp", "sp", "vp", "as", "bs", "ss", "vs")}
    for i in range(DEPTH):
        x = _ffn(x, _row(norm_ffn1[i]), wg1[i], wu1[i], wd1[i], gfin, False)
        p = _proj_in(x, _row(norm_mix[i]), w_in_b[i])
        ps = p[T_PROMPT:].reshape(DEC_SEQ, DEC_BATCH, N_PROJ)

        caw, cab = conv_a_w[i], _row(conv_a_b[i])
        nag, nab = _row(norm_a_g[i]), _row(norm_a_b[i])
        ya_p, a_p = _conv_a_prompt(p, caw, cab, nag, nab)
        ya_s, a_s = _conv_a_sample(ps, jnp.transpose(state_conv_a[:, i], (1, 0, 2)),
                                   caw, cab, nag, nab)
        a_s = jnp.transpose(a_s, (1, 0, 2))

        alog, dtb, og = _pad_lanes(a_log[i]), _pad_lanes(dt_bias[i]), _row(norm_o_g[i])
        yb_p, b_p, s_p = _delta_prompt(p, conv_b_w[i], alog, dtb, og)
        yb_s, b_s, s_s = _delta_sample(ps, jnp.transpose(state_conv_b[:, i], (1, 0, 2)),
                                       state_delta, i, conv_b_w[i], alog, dtb, og)
        b_s = jnp.transpose(b_s, (1, 0, 2))

        ncg, ncb = _row(norm_c_g[i]), _row(norm_c_b[i])
        yc_p, v_p = _gmlp_prompt(p, ncg, ncb, sgu_w[i], bst[i])
        yc_s, v_s = _gmlp_sample(ps, ncg, ncb, wsm[i], bsm[i])

        ya = jnp.concatenate([ya_p, ya_s.reshape(T_SAMPLE, D_A)], axis=0)
        yb = jnp.concatenate([yb_p, yb_s.reshape(T_SAMPLE, D_B)], axis=0)
        yc = jnp.concatenate([yc_p, yc_s.reshape(T_SAMPLE, D_C)], axis=0)
        x = _proj_out(x, ya, yb, yc, w_out_b[i])
        x = _ffn(x, _row(norm_ffn2[i]), wg2[i], wu2[i], wd2[i], gfin, i == DEPTH - 1)

        outs["ap"].append(a_p); outs["bp"].append(b_p); outs["sp"].append(s_p)
        outs["vp"].append(v_p); outs["as"].append(a_s); outs["bs"].append(b_s)
        outs["ss"].append(s_s); outs["vs"].append(jnp.transpose(v_s, (1, 0, 2)))

    y_prompt = x[:T_PROMPT].reshape(BATCH, SEQ, D_MODEL)
    y_sample = jnp.transpose(x[T_PROMPT:].reshape(DEC_SEQ, DEC_BATCH, D_MODEL), (1, 0, 2))
    st = lambda k: jnp.stack(outs[k], axis=1)
    return (y_prompt, y_sample, st("ap"), st("bp"), st("sp"), st("vp"),
            st("as"), st("bs"), st("ss"), st("vs"))
```

```python
import functools

import numpy as np
import jax
import jax.numpy as jnp
from jax import lax
from jax.experimental import pallas as pl
from jax.experimental.pallas import tpu as pltpu

F32 = jnp.float32
BF16 = jnp.bfloat16
HIGHEST = lax.Precision.HIGHEST

D_MODEL = 4096
BATCH = 4
SEQ = 2048
DEPTH = 2
DEC_BATCH = 128
DEC_SEQ = 4
D_A = 1024
D_B = 2048
D_C = 1024
DK = 128
DV = 128
H_B = 16
H_C = 8
HEAD_C = 128
CONV_A_W = 31
CONV_B_W = 4
CHUNK_C = 128
D_FF = 11008
EPS = 1e-6

T_PROMPT = BATCH * SEQ
T_SAMPLE = DEC_BATCH * DEC_SEQ
T_ALL = T_PROMPT + T_SAMPLE

N_PROJ = 12800
COL_BLK = 2048
SMALL_BLK = 512
LANES = 128
SUBLANES = 8

VMEM_LIMIT = 52 * 1024 * 1024

TM = 512
TF = 256
TN_IN = 1280
TN_OUT = 1024
TL_A = 256
TL_C = 256
CH_B = 128
HEAD_GROUP = 16
SB = 8


def _cparams(sem):
    return pltpu.CompilerParams(dimension_semantics=sem, vmem_limit_bytes=VMEM_LIMIT)


def _sigmoid(x):
    return jax.nn.sigmoid(x)


def _silu(x):
    return x * _sigmoid(x)


def _rms(x, g):
    return x * lax.rsqrt(jnp.mean(x * x, axis=-1, keepdims=True) + EPS) * g


def _layernorm(x, g, b):
    mu = jnp.mean(x, axis=-1, keepdims=True)
    xc = x - mu
    var = jnp.mean(xc * xc, axis=-1, keepdims=True)
    return xc * lax.rsqrt(var + EPS) * g + b


def _dot(a, b, precision=None):
    return jnp.dot(a, b, preferred_element_type=F32, precision=precision)


def _dot_nt(a, b, precision=None):
    return lax.dot_general(a, b, (((1,), (1,)), ((), ())), preferred_element_type=F32,
                           precision=precision)


def _dot_tn(a, b, precision=None):
    return lax.dot_general(a, b, (((0,), (0,)), ((), ())), preferred_element_type=F32,
                           precision=precision)


N_PROMPT_TILES = T_PROMPT // TM
N_TILES = T_ALL // TM


def _ffn_body(*refs, split_in, split_out, final_norm):
    n_in = 2 if split_in else 1
    n_out = 2 if split_out else 1
    x_hbm = refs[:n_in]
    g_ref, wg_ref, wu_ref, wd_ref, gf_ref = refs[n_in:n_in + 5]
    o_hbm = refs[n_in + 5:n_in + 5 + n_out]
    acc_ref, h_ref, ld_sem, wb_sem = refs[n_in + 5 + n_out:]
    i = pl.program_id(0)
    j = pl.program_id(1)
    ni = pl.num_programs(0)
    nj = pl.num_programs(1)
    slot = i % 2

    def tile_dma(tile, sl, arrays, sem, to_vmem, op):
        def run(hbm, row0):
            rows = hbm.at[pl.ds(pl.multiple_of(row0, TM), TM)]
            src, dst = (rows, acc_ref.at[sl]) if to_vmem else (acc_ref.at[sl], rows)
            cp = pltpu.make_async_copy(src, dst, sem.at[sl])
            cp.start() if op == "start" else cp.wait()

        if len(arrays) == 2:
            @pl.when(tile < N_PROMPT_TILES)
            def _():
                run(arrays[0], tile * TM)

            @pl.when(tile >= N_PROMPT_TILES)
            def _():
                run(arrays[1], (tile - N_PROMPT_TILES) * TM)
        else:
            run(arrays[0], tile * TM)

    load = functools.partial(tile_dma, arrays=x_hbm, sem=ld_sem, to_vmem=True)
    writeback = functools.partial(tile_dma, arrays=o_hbm, sem=wb_sem, to_vmem=False)

    @pl.when((i == 0) & (j == 0))
    def _():
        load(i, slot, op="start")

    @pl.when(j == 0)
    def _():
        load(i, slot, op="wait")
        h_ref[...] = _rms(acc_ref[slot], g_ref[...]).astype(BF16)

    @pl.when((j == 1) & (i + 1 < ni))
    def _():
        @pl.when(i >= 1)
        def _():
            writeback(i - 1, 1 - slot, op="wait")

        load(i + 1, 1 - slot, op="start")

    h = h_ref[...]
    hg = _dot(h, wg_ref[...])
    hu = _dot(h, wu_ref[...])
    a = (0.5 * _silu(hg) * hu).astype(BF16)
    acc_ref[slot] += _dot(a, wd_ref[...])

    @pl.when(j == nj - 1)
    def _():
        if final_norm:
            acc_ref[slot] = _rms(acc_ref[slot], gf_ref[...])
        writeback(i, slot, op="start")

        @pl.when(i == ni - 1)
        def _():
            writeback(i - 1, 1 - slot, op="wait")
            writeback(i, slot, op="wait")


def _ffn(xs, g, wg, wu, wd, gf, *, split_out, final_norm):
    nf = wg.shape[1] // TF
    assert N_TILES >= 2 and nf >= 2
    split_in = len(xs) == 2
    if split_out:
        out_shape = [jax.ShapeDtypeStruct((T_PROMPT, D_MODEL), F32),
                     jax.ShapeDtypeStruct((T_SAMPLE, D_MODEL), F32)]
    else:
        out_shape = [jax.ShapeDtypeStruct((T_ALL, D_MODEL), F32)]
    hbm = pl.BlockSpec(memory_space=pl.ANY)
    return pl.pallas_call(
        functools.partial(_ffn_body, split_in=split_in, split_out=split_out,
                          final_norm=final_norm),
        grid=(N_TILES, nf),
        in_specs=[hbm] * len(xs) + [
            pl.BlockSpec((1, D_MODEL), lambda i, j: (0, 0)),
            pl.BlockSpec((D_MODEL, TF), lambda i, j: (0, j)),
            pl.BlockSpec((D_MODEL, TF), lambda i, j: (0, j)),
            pl.BlockSpec((TF, D_MODEL), lambda i, j: (j, 0)),
            pl.BlockSpec((1, D_MODEL), lambda i, j: (0, 0)),
        ],
        out_specs=[hbm] * len(out_shape),
        out_shape=out_shape,
        scratch_shapes=[
            pltpu.VMEM((2, TM, D_MODEL), F32),
            pltpu.VMEM((TM, D_MODEL), BF16),
            pltpu.SemaphoreType.DMA((2,)),
            pltpu.SemaphoreType.DMA((2,)),
        ],
        compiler_params=_cparams(("arbitrary", "arbitrary")),
        name="ffn_final" if final_norm else "ffn",
    )(*xs, g, wg, wu, wd, gf)


def _proj_in_body(x_ref, g_ref, w_ref, o_ref, h_ref):
    @pl.when(pl.program_id(1) == 0)
    def _():
        h_ref[...] = _rms(x_ref[...], g_ref[...]).astype(BF16)

    o_ref[...] = _dot(h_ref[...], w_ref[...])


def _proj_in(x, g, w):
    t = x.shape[0]
    return pl.pallas_call(
        _proj_in_body,
        grid=(t // TM, N_PROJ // TN_IN),
        in_specs=[
            pl.BlockSpec((TM, D_MODEL), lambda i, j: (i, 0), pipeline_mode=pl.Buffered(1)),
            pl.BlockSpec((1, D_MODEL), lambda i, j: (0, 0)),
            pl.BlockSpec((D_MODEL, TN_IN), lambda i, j: (0, j)),
        ],
        out_specs=pl.BlockSpec((TM, TN_IN), lambda i, j: (i, j)),
        out_shape=jax.ShapeDtypeStruct((t, N_PROJ), F32),
        scratch_shapes=[pltpu.VMEM((TM, D_MODEL), BF16)],
        compiler_params=_cparams(("parallel", "arbitrary")),
        name="proj_in",
    )(x, g, w)


def _proj_out_body(x_ref, yap_ref, ybp_ref, ycp_ref, yas_ref, ybs_ref, ycs_ref, w_ref, o_ref):
    def mix(ya_ref, yb_ref, yc_ref):
        acc = x_ref[...]
        acc += _dot(ya_ref[...], w_ref[0:D_A, :])
        acc += _dot(yb_ref[...], w_ref[D_A:D_A + D_B, :])
        acc += _dot(yc_ref[...], w_ref[D_A + D_B:, :])
        o_ref[...] = acc

    is_prompt = pl.program_id(0) < N_PROMPT_TILES

    @pl.when(is_prompt)
    def _():
        mix(yap_ref, ybp_ref, ycp_ref)

    @pl.when(jnp.logical_not(is_prompt))
    def _():
        mix(yas_ref, ybs_ref, ycs_ref)


def _proj_out(x, y_prompt, y_sample, w):
    assert T_SAMPLE == TM

    def prompt_spec(width):
        return pl.BlockSpec((TM, width), lambda i, j: (jnp.minimum(i, N_PROMPT_TILES - 1), 0))

    def sample_spec(width):
        return pl.BlockSpec((TM, width), lambda i, j: (0, 0))

    widths = (D_A, D_B, D_C)
    return pl.pallas_call(
        _proj_out_body,
        grid=(N_TILES, D_MODEL // TN_OUT),
        in_specs=[pl.BlockSpec((TM, TN_OUT), lambda i, j: (i, j))]
        + [prompt_spec(wd) for wd in widths] + [sample_spec(wd) for wd in widths]
        + [pl.BlockSpec((D_MODEL, TN_OUT), lambda i, j: (0, j))],
        out_specs=pl.BlockSpec((TM, TN_OUT), lambda i, j: (i, j)),
        out_shape=jax.ShapeDtypeStruct((T_ALL, D_MODEL), F32),
        compiler_params=_cparams(("parallel", "arbitrary")),
        name="proj_out",
    )(x, *y_prompt, *y_sample, w)


HIST = 32


def _conv_a_prompt_body(pa_ref, w_ref, b_ref, lg_ref, lb_ref, ya_ref, st_ref, ext_ref):
    l = pl.program_id(1)

    @pl.when(l == 0)
    def _():
        ext_ref[0:HIST, :] = jnp.zeros((HIST, D_A), F32)

    pa = pa_ref[...]
    ext_ref[HIST:HIST + TL_A, :] = pa[:, :D_A] * _sigmoid(pa[:, D_A:])
    off = HIST - (CONV_A_W - 1)
    acc = jnp.broadcast_to(b_ref[...], (TL_A, D_A))
    for w in range(CONV_A_W):
        acc = acc + ext_ref[off + w:off + w + TL_A, :] * w_ref[w:w + 1, :]
    ya_ref[...] = _silu(_layernorm(acc, lg_ref[...], lb_ref[...])).astype(BF16)

    @pl.when(l == pl.num_programs(1) - 1)
    def _():
        st_ref[0] = ext_ref[HIST + TL_A - (CONV_A_W - 1):HIST + TL_A, :]

    ext_ref[0:HIST, :] = ext_ref[TL_A:TL_A + HIST, :]


def _conv_a_prompt(p, w, b, lg, lb):
    nl = SEQ // TL_A
    return pl.pallas_call(
        _conv_a_prompt_body,
        grid=(BATCH, nl),
        in_specs=[
            pl.BlockSpec((TL_A, COL_BLK), lambda bi, l: (bi * nl + l, 0)),
            pl.BlockSpec((CONV_A_W, D_A), lambda bi, l: (0, 0)),
            pl.BlockSpec((1, D_A), lambda bi, l: (0, 0)),
            pl.BlockSpec((1, D_A), lambda bi, l: (0, 0)),
            pl.BlockSpec((1, D_A), lambda bi, l: (0, 0)),
        ],
        out_specs=[
            pl.BlockSpec((TL_A, D_A), lambda bi, l: (bi * nl + l, 0)),
            pl.BlockSpec((1, CONV_A_W - 1, D_A), lambda bi, l: (bi, 0, 0)),
        ],
        out_shape=[
            jax.ShapeDtypeStruct((T_PROMPT, D_A), BF16),
            jax.ShapeDtypeStruct((BATCH, CONV_A_W - 1, D_A), F32),
        ],
        scratch_shapes=[pltpu.VMEM((HIST + TL_A, D_A), F32)],
        compiler_params=_cparams(("parallel", "arbitrary")),
        name="conv_a_prompt",
    )(p, w, b, lg, lb)


def _conv_a_sample_body(pa_ref, st_ref, w_ref, b_ref, lg_ref, lb_ref, ya_ref, ns_ref):
    ns = CONV_A_W - 1
    ga = []
    for t in range(DEC_SEQ):
        pa = pa_ref[t]
        ga.append(pa[:, :D_A] * _sigmoid(pa[:, D_A:]))

    def ext_row(r):
        return st_ref[r] if r < ns else ga[r - ns]

    for t in range(DEC_SEQ):
        acc = jnp.broadcast_to(b_ref[...], (SB, D_A))
        for w in range(CONV_A_W):
            acc = acc + ext_row(t + w) * w_ref[w:w + 1, :]
        ya_ref[t] = _silu(_layernorm(acc, lg_ref[...], lb_ref[...])).astype(BF16)
    for r in range(ns):
        ns_ref[r] = ext_row(r + DEC_SEQ)


def _conv_a_sample(ps, state_t, w, b, lg, lb):
    ns = CONV_A_W - 1
    return pl.pallas_call(
        _conv_a_sample_body,
        grid=(DEC_BATCH // SB,),
        in_specs=[
            pl.BlockSpec((DEC_SEQ, SB, COL_BLK), lambda s: (0, s, 0)),
            pl.BlockSpec((ns, SB, D_A), lambda s: (0, s, 0)),
            pl.BlockSpec((CONV_A_W, D_A), lambda s: (0, 0)),
            pl.BlockSpec((1, D_A), lambda s: (0, 0)),
            pl.BlockSpec((1, D_A), lambda s: (0, 0)),
            pl.BlockSpec((1, D_A), lambda s: (0, 0)),
        ],
        out_specs=[
            pl.BlockSpec((DEC_SEQ, SB, D_A), lambda s: (0, s, 0)),
            pl.BlockSpec((ns, SB, D_A), lambda s: (0, s, 0)),
        ],
        out_shape=[
            jax.ShapeDtypeStruct((DEC_SEQ, DEC_BATCH, D_A), BF16),
            jax.ShapeDtypeStruct((ns, DEC_BATCH, D_A), F32),
        ],
        compiler_params=_cparams(("parallel",)),
        name="conv_a_sample",
    )(ps, state_t, w, b, lg, lb)


_SQRT_HALF = float(np.sqrt(0.5))


def _gelu(x):
    return 0.5 * x * (1.0 + lax.erf(x * _SQRT_HALF))


def _gmlp_prompt_body(puv_ref, ng_ref, nb_ref, ws_ref, bst_ref, yc_ref, cv_ref):
    l = pl.program_id(1)
    uv = _gelu(puv_ref[...])
    u = uv[:, :D_C]
    vv = _layernorm(uv[:, D_C:], ng_ref[...], nb_ref[...])
    row = lax.broadcasted_iota(jnp.int32, (CHUNK_C, CHUNK_C), 0)
    col = lax.broadcasted_iota(jnp.int32, (CHUNK_C, CHUNK_C), 1)
    vb = vv.astype(BF16)
    for h in range(H_C):
        ws = jnp.where(row >= col, ws_ref[h], 0.0).astype(BF16)
        hs = slice(h * HEAD_C, (h + 1) * HEAD_C)
        for c in range(TL_C // CHUNK_C):
            rs = slice(c * CHUNK_C, (c + 1) * CHUNK_C)
            mixed = _dot(ws, vb[rs, hs]) + bst_ref[:, h:h + 1]
            yc_ref[rs, hs] = (u[rs, hs] * mixed).astype(BF16)

    @pl.when(l == pl.num_programs(1) - 1)
    def _():
        cv_ref[0] = vv[TL_C - CHUNK_C:, :]


def _gmlp_prompt(p, ng, nb, ws, bst):
    nl = SEQ // TL_C
    return pl.pallas_call(
        _gmlp_prompt_body,
        grid=(BATCH, nl),
        in_specs=[
            pl.BlockSpec((TL_C, COL_BLK), lambda bi, l: (bi * nl + l, 5)),
            pl.BlockSpec((1, D_C), lambda bi, l: (0, 0)),
            pl.BlockSpec((1, D_C), lambda bi, l: (0, 0)),
            pl.BlockSpec((H_C, CHUNK_C, CHUNK_C), lambda bi, l: (0, 0, 0)),
            pl.BlockSpec((CHUNK_C, H_C), lambda bi, l: (0, 0)),
        ],
        out_specs=[
            pl.BlockSpec((TL_C, D_C), lambda bi, l: (bi * nl + l, 0)),
            pl.BlockSpec((1, CHUNK_C, D_C), lambda bi, l: (bi, 0, 0)),
        ],
        out_shape=[
            jax.ShapeDtypeStruct((T_PROMPT, D_C), BF16),
            jax.ShapeDtypeStruct((BATCH, CHUNK_C, D_C), F32),
        ],
        compiler_params=_cparams(("parallel", "arbitrary")),
        name="gmlp_prompt",
    )(p, ng, nb, ws, bst)


def _gmlp_sample_body(puv_ref, ng_ref, nb_ref, wsm_ref, bsm_ref, yc_ref, cv_ref):
    u, vv = [], []
    for t in range(DEC_SEQ):
        uv = _gelu(puv_ref[t])
        u.append(uv[:, :D_C])
        vt = _layernorm(uv[:, D_C:], ng_ref[...], nb_ref[...])
        vv.append(vt)
        cv_ref[t] = vt
    for i in range(DEC_SEQ):
        mixed = jnp.broadcast_to(bsm_ref[i:i + 1, :], (SB, D_C))
        for j in range(i + 1):
            mixed = mixed + wsm_ref[i * DEC_SEQ + j:i * DEC_SEQ + j + 1, :] * vv[j]
        yc_ref[i] = (u[i] * mixed).astype(BF16)


def _gmlp_sample(ps, ng, nb, wsm, bsm):
    return pl.pallas_call(
        _gmlp_sample_body,
        grid=(DEC_BATCH // SB,),
        in_specs=[
            pl.BlockSpec((DEC_SEQ, SB, COL_BLK), lambda s: (0, s, 5)),
            pl.BlockSpec((1, D_C), lambda s: (0, 0)),
            pl.BlockSpec((1, D_C), lambda s: (0, 0)),
            pl.BlockSpec((DEC_SEQ * DEC_SEQ, D_C), lambda s: (0, 0)),
            pl.BlockSpec((SUBLANES, D_C), lambda s: (0, 0)),
        ],
        out_specs=[
            pl.BlockSpec((DEC_SEQ, SB, D_C), lambda s: (0, s, 0)),
            pl.BlockSpec((DEC_SEQ, SB, D_C), lambda s: (0, s, 0)),
        ],
        out_shape=[
            jax.ShapeDtypeStruct((DEC_SEQ, DEC_BATCH, D_C), BF16),
            jax.ShapeDtypeStruct((DEC_SEQ, DEC_BATCH, D_C), F32),
        ],
        compiler_params=_cparams(("parallel",)),
        name="gmlp_sample",
    )(ps, ng, nb, wsm, bsm)


def _softplus(x):
    return jnp.maximum(x, 0.0) + jnp.log1p(jnp.exp(-jnp.abs(x)))


def _l2norm(x):
    return x * lax.rsqrt(jnp.sum(x * x, axis=-1, keepdims=True) + EPS)


def _bdot(a, b):
    return _dot(a.astype(BF16), b.astype(BF16))


def _unit_lower_inverses(lmats, row, col):
    n = lmats[0].shape[0]
    eye = jnp.where(row == col, 1.0, 0.0).astype(F32)
    blk8 = (row >> 3) == (col >> 3)
    negs = [jnp.where(blk8, -l, 0.0) for l in lmats]
    ts = [eye + m for m in negs]
    p2 = [_bdot(m, m) for m in negs]
    ts = [t + _bdot(t, p) for t, p in zip(ts, p2)]
    p4 = [_bdot(p, p) for p in p2]
    ts = [t + _bdot(t, p) for t, p in zip(ts, p4)]
    sh = 3
    while (1 << sh) < n:
        sel = ((row >> (sh + 1)) == (col >> (sh + 1))) & ((row >> sh) != (col >> sh))
        offs = [jnp.where(sel, l, 0.0) for l in lmats]
        mid = [_bdot(t, o) for t, o in zip(ts, offs)]
        ts = [t - _bdot(m, t) for t, m in zip(ts, mid)]
        sh += 1
    return ts


def _delta_prompt_body(q_ref, k_ref, v_ref, z_ref, db_ref, cw_ref, alog_ref, dtb_ref, og_ref,
                       yb_ref, cb_ref, so_ref,
                       ext_ref, s_ref, qs_ref, ks_ref, vs_ref, gc_ref, gr_ref, bc_ref, os_ref):
    c = pl.program_id(1)
    n = CH_B
    hist = SUBLANES

    @pl.when(c == 0)
    def _():
        ext_ref[0:hist, :] = jnp.zeros((hist, 3 * D_B), F32)
        s_ref[...] = jnp.zeros((H_B, DK, DV), F32)

    ext_ref[hist:hist + n, 0:D_B] = q_ref[...]
    ext_ref[hist:hist + n, D_B:2 * D_B] = k_ref[...]
    ext_ref[hist:hist + n, 2 * D_B:] = v_ref[...]

    @pl.when(c == pl.num_programs(1) - 1)
    def _():
        cb_ref[0] = ext_ref[hist + n - (CONV_B_W - 1):hist + n, :]

    off = hist - (CONV_B_W - 1)
    acc = ext_ref[off:off + n, :] * cw_ref[0:1, :]
    for w in range(1, CONV_B_W):
        acc = acc + ext_ref[off + w:off + w + n, :] * cw_ref[w:w + 1, :]
    qkv = _silu(acc)
    ext_ref[0:hist, :] = ext_ref[n:n + hist, :]

    row = lax.broadcasted_iota(jnp.int32, (n, n), 0)
    col = lax.broadcasted_iota(jnp.int32, (n, n), 1)
    incl = row >= col

    db = db_ref[:, 0:LANES]
    g = -jnp.exp(alog_ref[...]) * _softplus(db + dtb_ref[...])
    beta = _sigmoid(db)
    gcs = _dot(jnp.where(incl, 1.0, 0.0).astype(F32), g, HIGHEST)
    gcs_t = gcs.T
    for h in range(H_B):
        hs = slice(h * DK, (h + 1) * DK)
        qs_ref[h] = _l2norm(qkv[:, hs]) * (DK ** -0.5)
        ks_ref[h] = _l2norm(qkv[:, D_B + h * DK:D_B + (h + 1) * DK])
        vs_ref[h] = qkv[:, 2 * D_B + h * DV:2 * D_B + (h + 1) * DV]
        gc_ref[h] = jnp.broadcast_to(gcs[:, h:h + 1], (n, n))
        gr_ref[h] = jnp.broadcast_to(gcs_t[h:h + 1, :], (n, n))
        bc_ref[h] = jnp.broadcast_to(beta[:, H_B + h:H_B + h + 1], (n, n))

    strict = row > col

    def head_group(grp, carry):
        hh = [grp * HEAD_GROUP + j for j in range(HEAD_GROUP)]
        q = [qs_ref[h] for h in hh]
        k = [ks_ref[h] for h in hh]
        gcol = [gc_ref[h] for h in hh]
        bcol = [bc_ref[h] for h in hh]
        decay = [jnp.where(incl, jnp.exp(jnp.where(incl, gc - gr_ref[h], 0.0)), 0.0)
                 for gc, h in zip(gcol, hh)]
        gam = [jnp.exp(gc) for gc in gcol]
        glast = [gc[n - 1:n, :] for gc in gcol]
        qkk = [_dot_nt(jnp.concatenate([qi, ki], axis=0).astype(BF16), ki.astype(BF16))
               for qi, ki in zip(q, k)]
        lmat = [jnp.where(strict, b * x[n:] * d, 0.0) for b, x, d in zip(bcol, qkk, decay)]
        tinv = _unit_lower_inverses(lmat, row, col)
        sol = [_bdot(t, jnp.concatenate([b * vs_ref[h], (b * gm) * ki], axis=1))
               for t, b, h, gm, ki in zip(tinv, bcol, hh, gam, k)]
        for j, h in enumerate(hh):
            s = s_ref[h]
            r = _bdot(jnp.concatenate([sol[j][:, DV:], q[j] * gam[j]], axis=0), s)
            ub = (sol[j][:, :DV] - r[:n]).astype(BF16)
            o = r[n:] + _dot((qkk[j][:n] * decay[j]).astype(BF16), ub)
            kd = k[j] * jnp.exp(glast[j] - gcol[j])
            s_ref[h] = jnp.exp(glast[j]) * s + _dot_tn(kd.astype(BF16), ub)
            os_ref[h] = _rms(o, og_ref[...])
        return carry

    lax.fori_loop(0, H_B // HEAD_GROUP, head_group, 0)

    for h in range(H_B):
        hs = slice(h * DV, (h + 1) * DV)
        yb_ref[:, hs] = (os_ref[h] * _silu(z_ref[:, hs])).astype(BF16)

    @pl.when(c == pl.num_programs(1) - 1)
    def _():
        so_ref[0] = s_ref[...]


def _delta_prompt(p, cw, alog, dtb, og):
    nc = SEQ // CH_B
    pad_col = (N_PROJ - SMALL_BLK) // SMALL_BLK
    tile = pltpu.VMEM((H_B, CH_B, CH_B), F32)
    return pl.pallas_call(
        _delta_prompt_body,
        grid=(BATCH, nc),
        in_specs=[
            pl.BlockSpec((CH_B, COL_BLK), lambda bi, c: (bi * nc + c, 1)),
            pl.BlockSpec((CH_B, COL_BLK), lambda bi, c: (bi * nc + c, 2)),
            pl.BlockSpec((CH_B, COL_BLK), lambda bi, c: (bi * nc + c, 3)),
            pl.BlockSpec((CH_B, COL_BLK), lambda bi, c: (bi * nc + c, 4)),
            pl.BlockSpec((CH_B, SMALL_BLK), lambda bi, c: (bi * nc + c, pad_col)),
            pl.BlockSpec((CONV_B_W, 3 * D_B), lambda bi, c: (0, 0)),
            pl.BlockSpec((1, LANES), lambda bi, c: (0, 0)),
            pl.BlockSpec((1, LANES), lambda bi, c: (0, 0)),
            pl.BlockSpec((1, DV), lambda bi, c: (0, 0)),
        ],
        out_specs=[
            pl.BlockSpec((CH_B, D_B), lambda bi, c: (bi * nc + c, 0)),
            pl.BlockSpec((1, CONV_B_W - 1, 3 * D_B), lambda bi, c: (bi, 0, 0)),
            pl.BlockSpec((1, H_B, DK, DV), lambda bi, c: (bi, 0, 0, 0)),
        ],
        out_shape=[
            jax.ShapeDtypeStruct((T_PROMPT, D_B), BF16),
            jax.ShapeDtypeStruct((BATCH, CONV_B_W - 1, 3 * D_B), F32),
            jax.ShapeDtypeStruct((BATCH, H_B, DK, DV), F32),
        ],
        scratch_shapes=[
            pltpu.VMEM((SUBLANES + CH_B, 3 * D_B), F32),
            pltpu.VMEM((H_B, DK, DV), F32),
            tile, tile, tile, tile, tile, tile, tile,
        ],
        compiler_params=_cparams(("parallel", "arbitrary")),
        name="delta_prompt",
    )(p, p, p, p, p, cw, alog, dtb, og)


def _delta_sample_body(q_ref, k_ref, v_ref, z_ref, db_ref, st_ref, s0_ref, cw_ref, alog_ref,
                       dtb_ref, og_ref, yb_ref, cb_ref, so_ref):
    ns = CONV_B_W - 1
    nt = DEC_SEQ

    def ext_part(r, lo, ref):
        return st_ref[r, :, lo:lo + D_B] if r < ns else ref[r - ns]

    def conv(lo, ref):
        out = []
        for t in range(nt):
            acc = ext_part(t, lo, ref) * cw_ref[0:1, lo:lo + D_B]
            for w in range(1, CONV_B_W):
                acc = acc + ext_part(t + w, lo, ref) * cw_ref[w:w + 1, lo:lo + D_B]
            out.append(_silu(acc))
        return out

    qc, kc, vc = conv(0, q_ref), conv(D_B, k_ref), conv(2 * D_B, v_ref)
    for r in range(ns):
        for lo, ref in ((0, q_ref), (D_B, k_ref), (2 * D_B, v_ref)):
            cb_ref[r, :, lo:lo + D_B] = ext_part(r + nt, lo, ref)

    g, beta = [], []
    for t in range(nt):
        db = db_ref[t][:, 0:LANES]
        g.append(-jnp.exp(alog_ref[...]) * _softplus(db + dtb_ref[...]))
        beta.append(_sigmoid(db))
    gc = [g[0]]
    for t in range(1, nt):
        gc.append(gc[t - 1] + g[t])

    rowid = lax.broadcasted_iota(jnp.int32, (SB, DV), 0)
    rows4 = lax.broadcasted_iota(jnp.int32, (nt * SB, DV), 0) & (SB - 1)

    for h in range(H_B):
        hs = slice(h * DK, (h + 1) * DK)
        q = [_l2norm(qc[t][:, hs]) * (DK ** -0.5) for t in range(nt)]
        k = [_l2norm(kc[t][:, hs]) for t in range(nt)]
        v = [vc[t][:, hs] for t in range(nt)]
        gch = [gc[t][:, h:h + 1] for t in range(nt)]
        bh = [beta[t][:, H_B + h:H_B + h + 1] for t in range(nt)]
        gam = [jnp.exp(x) for x in gch]

        lhs = jnp.concatenate(k + [q[t] * gam[t] for t in range(nt)], axis=0).astype(BF16)
        ks0 = [jnp.zeros((SB, DV), F32) for _ in range(nt)]
        qs0 = [jnp.zeros((SB, DV), F32) for _ in range(nt)]
        for s in range(SB):
            res = _dot(lhs, s0_ref[s, h].astype(BF16))
            for t in range(nt):
                ks0[t] = jnp.where(rowid == s, res[t * SB:(t + 1) * SB], ks0[t])
                qs0[t] = jnp.where(rowid == s, res[(nt + t) * SB:(nt + t + 1) * SB], qs0[t])

        u = []
        for t in range(nt):
            acc = bh[t] * (v[t] - gam[t] * ks0[t])
            for j in range(t):
                kk = jnp.sum(k[t] * k[j], axis=-1, keepdims=True)
                acc = acc - (bh[t] * kk * jnp.exp(gch[t] - gch[j])) * u[j]
            u.append(acc)

        for t in range(nt):
            o = qs0[t]
            for j in range(t + 1):
                qk = jnp.sum(q[t] * k[j], axis=-1, keepdims=True)
                o = o + (qk * jnp.exp(gch[t] - gch[j])) * u[j]
            yb_ref[t, :, hs] = (_rms(o, og_ref[...]) * _silu(z_ref[t][:, hs])).astype(BF16)

        kd = jnp.concatenate([k[t] * jnp.exp(gch[nt - 1] - gch[t]) for t in range(nt)],
                             axis=0).astype(BF16)
        uall = jnp.concatenate(u, axis=0)
        for s in range(SB):
            um = jnp.where(rows4 == s, uall, 0.0).astype(BF16)
            gl = jnp.exp(gch[nt - 1][s:s + 1, :])
            so_ref[s, h] = gl * s0_ref[s, h] + _dot_tn(kd, um)


def _delta_sample(ps, conv_state_t, delta_state, layer, cw, alog, dtb, og):
    ns = CONV_B_W - 1
    pad_col = (N_PROJ - SMALL_BLK) // SMALL_BLK
    return pl.pallas_call(
        _delta_sample_body,
        grid=(DEC_BATCH // SB,),
        in_specs=[
            pl.BlockSpec((DEC_SEQ, SB, COL_BLK), lambda s: (0, s, 1)),
            pl.BlockSpec((DEC_SEQ, SB, COL_BLK), lambda s: (0, s, 2)),
            pl.BlockSpec((DEC_SEQ, SB, COL_BLK), lambda s: (0, s, 3)),
            pl.BlockSpec((DEC_SEQ, SB, COL_BLK), lambda s: (0, s, 4)),
            pl.BlockSpec((DEC_SEQ, SB, SMALL_BLK), lambda s: (0, s, pad_col)),
            pl.BlockSpec((ns, SB, 3 * D_B), lambda s: (0, s, 0)),
            pl.BlockSpec((SB, None, H_B, DK, DV), lambda s: (s, layer, 0, 0, 0)),
            pl.BlockSpec((CONV_B_W, 3 * D_B), lambda s: (0, 0)),
            pl.BlockSpec((1, LANES), lambda s: (0, 0)),
            pl.BlockSpec((1, LANES), lambda s: (0, 0)),
            pl.BlockSpec((1, DV), lambda s: (0, 0)),
        ],
        out_specs=[
            pl.BlockSpec((DEC_SEQ, SB, D_B), lambda s: (0, s, 0)),
            pl.BlockSpec((ns, SB, 3 * D_B), lambda s: (0, s, 0)),
            pl.BlockSpec((SB, H_B, DK, DV), lambda s: (s, 0, 0, 0)),
        ],
        out_shape=[
            jax.ShapeDtypeStruct((DEC_SEQ, DEC_BATCH, D_B), BF16),
            jax.ShapeDtypeStruct((ns, DEC_BATCH, 3 * D_B), F32),
            jax.ShapeDtypeStruct((DEC_BATCH, H_B, DK, DV), F32),
        ],
        compiler_params=_cparams(("parallel",)),
        name="delta_sample",
    )(ps, ps, ps, ps, ps, conv_state_t, delta_state, cw, alog, dtb, og)


def _prep_w_in(w_in):
    a0, a1 = 0, 2 * D_A
    q1 = a1 + 3 * D_B
    z1 = q1 + D_B
    d1 = z1 + 2 * H_B
    pad = jnp.zeros(w_in.shape[:-1] + (SMALL_BLK - 2 * H_B,), w_in.dtype)
    cols = [w_in[..., a0:a1], w_in[..., a1:q1], w_in[..., q1:z1], w_in[..., d1:],
            w_in[..., z1:d1], pad]
    return jnp.concatenate(cols, axis=-1).astype(BF16)


def _row(x):
    return x.reshape(1, -1)


def _pad_lanes(x):
    return jnp.pad(x, (0, LANES - x.shape[0])).reshape(1, LANES)


def kernel(x_prompt, x_sample, state_conv_a, state_conv_b, state_delta, norm_ffn1, ffn1_gate,
           ffn1_up, ffn1_down, norm_mix, w_in, conv_a_w, conv_a_b, norm_a_g, norm_a_b, conv_b_w,
           a_log, dt_bias, norm_o_g, norm_c_g, norm_c_b, sgu_w, sgu_b, w_out, norm_ffn2,
           ffn2_gate, ffn2_up, ffn2_down, norm_final):
    def bf(w):
        return w.astype(BF16)

    x = [x_prompt.reshape(T_PROMPT, D_MODEL),
         jnp.transpose(x_sample, (1, 0, 2)).reshape(T_SAMPLE, D_MODEL)]
    gfin = _row(norm_final)

    wsm = jnp.repeat(sgu_w[:, :, :DEC_SEQ, :DEC_SEQ].reshape(DEPTH, H_C, DEC_SEQ * DEC_SEQ),
                     HEAD_C, axis=1)
    wsm = jnp.transpose(wsm, (0, 2, 1))
    bsm = jnp.transpose(jnp.repeat(sgu_b[:, :, :SUBLANES], HEAD_C, axis=1), (0, 2, 1))
    bst = jnp.transpose(sgu_b, (0, 2, 1))

    outs = {k: [] for k in ("ap", "bp", "sp", "vp", "as", "bs", "ss", "vs")}
    for i in range(DEPTH):
        x = _ffn(x, _row(norm_ffn1[i]), bf(ffn1_gate[i]), bf(ffn1_up[i]), bf(ffn1_down[i]), gfin,
                 split_out=False, final_norm=False)[0]
        p = _proj_in(x, _row(norm_mix[i]), _prep_w_in(w_in[i]))
        ps = p[T_PROMPT:].reshape(DEC_SEQ, DEC_BATCH, N_PROJ)

        caw, cab = conv_a_w[i], _row(conv_a_b[i])
        nag, nab = _row(norm_a_g[i]), _row(norm_a_b[i])
        ya_p, a_p = _conv_a_prompt(p, caw, cab, nag, nab)
        ya_s, a_s = _conv_a_sample(ps, jnp.transpose(state_conv_a[:, i], (1, 0, 2)),
                                   caw, cab, nag, nab)
        a_s = jnp.transpose(a_s, (1, 0, 2))

        alog, dtb, og = _pad_lanes(a_log[i]), _pad_lanes(dt_bias[i]), _row(norm_o_g[i])
        yb_p, b_p, s_p = _delta_prompt(p, conv_b_w[i], alog, dtb, og)
        yb_s, b_s, s_s = _delta_sample(ps, jnp.transpose(state_conv_b[:, i], (1, 0, 2)),
                                       state_delta, i, conv_b_w[i], alog, dtb, og)
        b_s = jnp.transpose(b_s, (1, 0, 2))

        ncg, ncb = _row(norm_c_g[i]), _row(norm_c_b[i])
        yc_p, v_p = _gmlp_prompt(p, ncg, ncb, sgu_w[i], bst[i])
        yc_s, v_s = _gmlp_sample(ps, ncg, ncb, wsm[i], bsm[i])

        y_s = [y.reshape(T_SAMPLE, y.shape[-1]) for y in (ya_s, yb_s, yc_s)]
        x = _proj_out(x, [ya_p, yb_p, yc_p], y_s, bf(w_out[i]))
        last = i == DEPTH - 1
        x = _ffn([x], _row(norm_ffn2[i]), bf(ffn2_gate[i]), bf(ffn2_up[i]), bf(ffn2_down[i]),
                 gfin, split_out=last, final_norm=last)

        outs["ap"].append(a_p); outs["bp"].append(b_p); outs["sp"].append(s_p)
        outs["vp"].append(v_p); outs["as"].append(a_s); outs["bs"].append(b_s)
        outs["ss"].append(s_s); outs["vs"].append(jnp.transpose(v_s, (1, 0, 2)))

    y_prompt = x[0].reshape(BATCH, SEQ, D_MODEL)
    y_sample = jnp.transpose(x[1].reshape(DEC_SEQ, DEC_BATCH, D_MODEL), (1, 0, 2))
    st = lambda k: jnp.stack(outs[k], axis=1)
    return (y_prompt, y_sample, st("ap"), st("bp"), st("sp"), st("vp"),
            st("as"), st("bs"), st("ss"), st("vs"))
```

```python
import functools

import numpy as np
import jax
import jax.numpy as jnp
from jax import lax
from jax.experimental import pallas as pl
from jax.experimental.pallas import tpu as pltpu

F32 = jnp.float32
BF16 = jnp.bfloat16
HIGHEST = lax.Precision.HIGHEST

D_MODEL = 4096
BATCH = 4
SEQ = 2048
DEPTH = 2
DEC_BATCH = 128
DEC_SEQ = 4
D_A = 1024
D_B = 2048
D_C = 1024
DK = 128
DV = 128
H_B = 16
H_C = 8
HEAD_C = 128
CONV_A_W = 31
CONV_B_W = 4
CHUNK_C = 128
D_FF = 11008
EPS = 1e-6

T_PROMPT = BATCH * SEQ
T_SAMPLE = DEC_BATCH * DEC_SEQ
T_ALL = T_PROMPT + T_SAMPLE

N_IN = 2 * D_A + 4 * D_B + 2 * H_B + 2 * D_C
N_MAIN = 2 * D_A + 4 * D_B
COL_BLK = 2048
SMALL_BLK = 512
TN_IN = 512
N_PROJ = N_MAIN + 2 * D_C + TN_IN
UV_COL_BLK = N_MAIN // COL_BLK
DB_COL_BLK = (N_MAIN + 2 * D_C) // SMALL_BLK
LANES = 128
SUBLANES = 8

VMEM_LIMIT = 52 * 1024 * 1024

TM = 512
TF = 256
TN_OUT = 1024
TL_A = 256
TL_C = 256
CH_B = 128
HEAD_GROUP = 16
SB = 8


def _cparams(sem):
    return pltpu.CompilerParams(dimension_semantics=sem, vmem_limit_bytes=VMEM_LIMIT)


def _sigmoid(x):
    return jax.nn.sigmoid(x)


def _silu(x):
    return x * _sigmoid(x)


def _rms(x, g):
    return x * lax.rsqrt(jnp.mean(x * x, axis=-1, keepdims=True) + EPS) * g


def _layernorm(x, g, b):
    mu = jnp.mean(x, axis=-1, keepdims=True)
    xc = x - mu
    var = jnp.mean(xc * xc, axis=-1, keepdims=True)
    return xc * lax.rsqrt(var + EPS) * g + b


def _dot(a, b, precision=None):
    return jnp.dot(a, b, preferred_element_type=F32, precision=precision)


def _dot_nt(a, b, precision=None):
    return lax.dot_general(a, b, (((1,), (1,)), ((), ())), preferred_element_type=F32,
                           precision=precision)


def _dot_tn(a, b, precision=None):
    return lax.dot_general(a, b, (((0,), (0,)), ((), ())), preferred_element_type=F32,
                           precision=precision)


N_PROMPT_TILES = T_PROMPT // TM
N_TILES = T_ALL // TM


def _ffn_body(*refs, split_in, split_out, layer, final_norm):
    n_in = 2 if split_in else 1
    n_out = 2 if split_out else 1
    x_hbm = refs[:n_in]
    g_ref, wg_hbm, wu_hbm, wd_hbm, gf_ref = refs[n_in:n_in + 5]
    o_hbm = refs[n_in + 5:n_in + 5 + n_out]
    (acc_ref, h_ref, wg_buf, wu_buf, wd_buf, ld_sem, wb_sem,
     w_sem) = refs[n_in + 5 + n_out:]
    i = pl.program_id(0)
    ni = pl.num_programs(0)
    nf = wg_hbm.shape[2] // TF
    slot = i % 2

    def tile_dma(tile, sl, arrays, sem, to_vmem, op):
        def run(hbm, row0):
            rows = hbm.at[pl.ds(pl.multiple_of(row0, TM), TM)]
            src, dst = (rows, acc_ref.at[sl]) if to_vmem else (acc_ref.at[sl], rows)
            cp = pltpu.make_async_copy(src, dst, sem.at[sl])
            cp.start() if op == "start" else cp.wait()

        if len(arrays) == 2:
            @pl.when(tile < N_PROMPT_TILES)
            def _():
                run(arrays[0], tile * TM)

            @pl.when(tile >= N_PROMPT_TILES)
            def _():
                run(arrays[1], (tile - N_PROMPT_TILES) * TM)
        else:
            run(arrays[0], tile * TM)

    load = functools.partial(tile_dma, arrays=x_hbm, sem=ld_sem, to_vmem=True)
    writeback = functools.partial(tile_dma, arrays=o_hbm, sem=wb_sem, to_vmem=False)

    def weight_dma(jt, ws, op):
        c0 = pl.multiple_of(jt * TF, TF)
        copies = (
            (wg_hbm.at[layer, :, pl.ds(c0, TF)], wg_buf.at[ws], w_sem.at[0, ws]),
            (wu_hbm.at[layer, :, pl.ds(c0, TF)], wu_buf.at[ws], w_sem.at[1, ws]),
            (wd_hbm.at[layer, pl.ds(c0, TF), :], wd_buf.at[ws], w_sem.at[2, ws]),
        )
        for src, dst, sem in copies:
            cp = pltpu.make_async_copy(src, dst, sem)
            cp.start() if op == "start" else cp.wait()

    @pl.when(i == 0)
    def _():
        load(i, slot, op="start")
        weight_dma(0, 0, "start")

    load(i, slot, op="wait")
    h_ref[...] = _rms(acc_ref[slot], g_ref[...]).astype(BF16)

    def step(j, carry):
        ws = (i * nf + j) % 2
        weight_dma(j, ws, "wait")
        more = j + 1 < nf

        @pl.when(more | (i + 1 < ni))
        def _():
            weight_dma(jnp.where(more, j + 1, 0), 1 - ws, "start")

        @pl.when((j == 1) & (i + 1 < ni))
        def _():
            @pl.when(i >= 1)
            def _():
                writeback(i - 1, 1 - slot, op="wait")

            load(i + 1, 1 - slot, op="start")

        h = h_ref[...]
        hg = _dot(h, wg_buf[ws])
        hu = _dot(h, wu_buf[ws])
        a = (0.5 * _silu(hg) * hu).astype(BF16)
        acc_ref[slot] += _dot(a, wd_buf[ws])
        return carry

    lax.fori_loop(0, nf, step, 0)

    if final_norm:
        acc_ref[slot] = _rms(acc_ref[slot], gf_ref[...])
    writeback(i, slot, op="start")

    @pl.when(i == ni - 1)
    def _():
        writeback(i - 1, 1 - slot, op="wait")
        writeback(i, slot, op="wait")


def _ffn(xs, g, wg, wu, wd, gf, *, layer, split_out, final_norm):
    assert N_TILES >= 2 and wg.shape[2] % TF == 0 and wg.shape[2] // TF >= 2
    split_in = len(xs) == 2
    if split_out:
        out_shape = [jax.ShapeDtypeStruct((T_PROMPT, D_MODEL), F32),
                     jax.ShapeDtypeStruct((T_SAMPLE, D_MODEL), F32)]
    else:
        out_shape = [jax.ShapeDtypeStruct((T_ALL, D_MODEL), F32)]
    hbm = pl.BlockSpec(memory_space=pl.ANY)
    vec = pl.BlockSpec((1, D_MODEL), lambda i: (0, 0))
    return pl.pallas_call(
        functools.partial(_ffn_body, split_in=split_in, split_out=split_out, layer=layer,
                          final_norm=final_norm),
        grid=(N_TILES,),
        in_specs=[hbm] * len(xs) + [vec, hbm, hbm, hbm, vec],
        out_specs=[hbm] * len(out_shape),
        out_shape=out_shape,
        scratch_shapes=[
            pltpu.VMEM((2, TM, D_MODEL), F32),
            pltpu.VMEM((TM, D_MODEL), BF16),
            pltpu.VMEM((2, D_MODEL, TF), BF16),
            pltpu.VMEM((2, D_MODEL, TF), BF16),
            pltpu.VMEM((2, TF, D_MODEL), BF16),
            pltpu.SemaphoreType.DMA((2,)),
            pltpu.SemaphoreType.DMA((2,)),
            pltpu.SemaphoreType.DMA((3, 2)),
        ],
        compiler_params=_cparams(("arbitrary",)),
        name="ffn_final" if final_norm else "ffn",
    )(*xs, g, wg, wu, wd, gf)


N_MAIN_TILES = N_MAIN // TN_IN
N_UV_TILES = 2 * D_C // TN_IN


def _proj_in_body(x_ref, g_ref, w_ref, wuv_ref, o_ref, h_ref):
    j = pl.program_id(1)

    @pl.when(j == 0)
    def _():
        h_ref[...] = _rms(x_ref[...], g_ref[...]).astype(BF16)

    is_uv = (j >= N_MAIN_TILES) & (j < N_MAIN_TILES + N_UV_TILES)

    @pl.when(is_uv)
    def _():
        o_ref[...] = _dot(h_ref[...], wuv_ref[...])

    @pl.when(jnp.logical_not(is_uv))
    def _():
        o_ref[...] = _dot(h_ref[...], w_ref[...])


def _proj_in(x, g, w, w_uv, layer):
    t = x.shape[0]

    def main_tile(i, j):
        return (layer, 0, jnp.minimum(j, N_MAIN_TILES))

    def uv_tile(i, j):
        return (layer, 0, jnp.clip(j - N_MAIN_TILES, 0, N_UV_TILES - 1))

    return pl.pallas_call(
        _proj_in_body,
        grid=(t // TM, N_PROJ // TN_IN),
        in_specs=[
            pl.BlockSpec((TM, D_MODEL), lambda i, j: (i, 0), pipeline_mode=pl.Buffered(1)),
            pl.BlockSpec((1, D_MODEL), lambda i, j: (0, 0)),
            pl.BlockSpec((None, D_MODEL, TN_IN), main_tile),
            pl.BlockSpec((None, D_MODEL, TN_IN), uv_tile),
        ],
        out_specs=pl.BlockSpec((TM, TN_IN), lambda i, j: (i, j)),
        out_shape=jax.ShapeDtypeStruct((t, N_PROJ), F32),
        scratch_shapes=[pltpu.VMEM((TM, D_MODEL), BF16)],
        compiler_params=_cparams(("parallel", "arbitrary")),
        name="proj_in",
    )(x, g, w, w_uv)


def _proj_out_body(x_ref, yap_ref, ybp_ref, ycp_ref, yas_ref, ybs_ref, ycs_ref, w_ref, o_ref):
    def mix(ya_ref, yb_ref, yc_ref):
        acc = x_ref[...]
        acc += _dot(ya_ref[...], w_ref[0:D_A, :])
        acc += _dot(yb_ref[...], w_ref[D_A:D_A + D_B, :])
        acc += _dot(yc_ref[...], w_ref[D_A + D_B:, :])
        o_ref[...] = acc

    is_prompt = pl.program_id(0) < N_PROMPT_TILES

    @pl.when(is_prompt)
    def _():
        mix(yap_ref, ybp_ref, ycp_ref)

    @pl.when(jnp.logical_not(is_prompt))
    def _():
        mix(yas_ref, ybs_ref, ycs_ref)


def _proj_out(x, y_prompt, y_sample, w, layer):
    assert T_SAMPLE == TM

    def prompt_spec(width):
        return pl.BlockSpec((TM, width), lambda i, j: (jnp.minimum(i, N_PROMPT_TILES - 1), 0))

    def sample_spec(width):
        return pl.BlockSpec((TM, width), lambda i, j: (0, 0))

    widths = (D_A, D_B, D_C)
    return pl.pallas_call(
        _proj_out_body,
        grid=(N_TILES, D_MODEL // TN_OUT),
        in_specs=[pl.BlockSpec((TM, TN_OUT), lambda i, j: (i, j))]
        + [prompt_spec(wd) for wd in widths] + [sample_spec(wd) for wd in widths]
        + [pl.BlockSpec((None, D_MODEL, TN_OUT), lambda i, j: (layer, 0, j))],
        out_specs=pl.BlockSpec((TM, TN_OUT), lambda i, j: (i, j)),
        out_shape=jax.ShapeDtypeStruct((T_ALL, D_MODEL), F32),
        compiler_params=_cparams(("parallel", "arbitrary")),
        name="proj_out",
    )(x, *y_prompt, *y_sample, w)


HIST = 32


RB_A = 32


def _conv_a_prompt_body(pa_ref, w_ref, b_ref, lg_ref, lb_ref, ya_ref, st_ref,
                        ext_ref, sh_ref, conv_ref):
    l = pl.program_id(1)

    @pl.when(l == 0)
    def _():
        ext_ref[0:HIST, :] = jnp.zeros((HIST, D_A), F32)

    pa = pa_ref[...]
    ext_ref[HIST:HIST + TL_A, :] = pa[:, :D_A] * _sigmoid(pa[:, D_A:])
    n_sh = HIST + TL_A - SUBLANES
    for r in range(1, SUBLANES):
        sh_ref[r - 1] = ext_ref[r:r + n_sh, :]
    off = HIST - (CONV_A_W - 1)

    def block(rb, carry):
        base = rb * RB_A
        acc = jnp.broadcast_to(b_ref[...], (RB_A, D_A))
        for w in range(CONV_A_W):
            r = (off + w) % SUBLANES
            rows = pl.ds(pl.multiple_of(base + (off + w - r), SUBLANES), RB_A)
            x = ext_ref[rows, :] if r == 0 else sh_ref[r - 1, rows, :]
            acc = acc + x * w_ref[w:w + 1, :]
        conv_ref[pl.ds(pl.multiple_of(base, RB_A), RB_A), :] = acc
        return carry

    lax.fori_loop(0, TL_A // RB_A, block, 0)
    ya_ref[...] = _silu(_layernorm(conv_ref[...], lg_ref[...], lb_ref[...])).astype(BF16)

    @pl.when(l == pl.num_programs(1) - 1)
    def _():
        st_ref[0] = ext_ref[HIST + TL_A - (CONV_A_W - 1):HIST + TL_A, :]

    ext_ref[0:HIST, :] = ext_ref[TL_A:TL_A + HIST, :]


def _conv_a_prompt(p, w, b, lg, lb):
    nl = SEQ // TL_A
    return pl.pallas_call(
        _conv_a_prompt_body,
        grid=(BATCH, nl),
        in_specs=[
            pl.BlockSpec((TL_A, COL_BLK), lambda bi, l: (bi * nl + l, 0)),
            pl.BlockSpec((CONV_A_W, D_A), lambda bi, l: (0, 0)),
            pl.BlockSpec((1, D_A), lambda bi, l: (0, 0)),
            pl.BlockSpec((1, D_A), lambda bi, l: (0, 0)),
            pl.BlockSpec((1, D_A), lambda bi, l: (0, 0)),
        ],
        out_specs=[
            pl.BlockSpec((TL_A, D_A), lambda bi, l: (bi * nl + l, 0)),
            pl.BlockSpec((1, CONV_A_W - 1, D_A), lambda bi, l: (bi, 0, 0)),
        ],
        out_shape=[
            jax.ShapeDtypeStruct((T_PROMPT, D_A), BF16),
            jax.ShapeDtypeStruct((BATCH, CONV_A_W - 1, D_A), F32),
        ],
        scratch_shapes=[
            pltpu.VMEM((HIST + TL_A, D_A), F32),
            pltpu.VMEM((SUBLANES - 1, HIST + TL_A - SUBLANES, D_A), F32),
            pltpu.VMEM((TL_A, D_A), F32),
        ],
        compiler_params=_cparams(("parallel", "arbitrary")),
        name="conv_a_prompt",
    )(p, w, b, lg, lb)


def _conv_a_sample_body(pa_ref, st_ref, w_ref, b_ref, lg_ref, lb_ref, ya_ref, ns_ref):
    ns = CONV_A_W - 1
    ga = []
    for t in range(DEC_SEQ):
        pa = pa_ref[t]
        ga.append(pa[:, :D_A] * _sigmoid(pa[:, D_A:]))

    def ext_row(r):
        return st_ref[r] if r < ns else ga[r - ns]

    for t in range(DEC_SEQ):
        acc = jnp.broadcast_to(b_ref[...], (SB, D_A))
        for w in range(CONV_A_W):
            acc = acc + ext_row(t + w) * w_ref[w:w + 1, :]
        ya_ref[t] = _silu(_layernorm(acc, lg_ref[...], lb_ref[...])).astype(BF16)
    for r in range(ns):
        ns_ref[r] = ext_row(r + DEC_SEQ)


def _conv_a_sample(ps, state_t, w, b, lg, lb):
    ns = CONV_A_W - 1
    return pl.pallas_call(
        _conv_a_sample_body,
        grid=(DEC_BATCH // SB,),
        in_specs=[
            pl.BlockSpec((DEC_SEQ, SB, COL_BLK), lambda s: (0, s, 0)),
            pl.BlockSpec((ns, SB, D_A), lambda s: (0, s, 0)),
            pl.BlockSpec((CONV_A_W, D_A), lambda s: (0, 0)),
            pl.BlockSpec((1, D_A), lambda s: (0, 0)),
            pl.BlockSpec((1, D_A), lambda s: (0, 0)),
            pl.BlockSpec((1, D_A), lambda s: (0, 0)),
        ],
        out_specs=[
            pl.BlockSpec((DEC_SEQ, SB, D_A), lambda s: (0, s, 0)),
            pl.BlockSpec((ns, SB, D_A), lambda s: (0, s, 0)),
        ],
        out_shape=[
            jax.ShapeDtypeStruct((DEC_SEQ, DEC_BATCH, D_A), BF16),
            jax.ShapeDtypeStruct((ns, DEC_BATCH, D_A), F32),
        ],
        compiler_params=_cparams(("parallel",)),
        name="conv_a_sample",
    )(ps, state_t, w, b, lg, lb)


_SQRT_HALF = float(np.sqrt(0.5))


def _gelu(x):
    return 0.5 * x * (1.0 + lax.erf(x * _SQRT_HALF))


def _gmlp_prompt_body(puv_ref, ng_ref, nb_ref, ws_ref, bst_ref, yc_ref, cv_ref):
    l = pl.program_id(1)
    uv = _gelu(puv_ref[...])
    u = uv[:, :D_C]
    vv = _layernorm(uv[:, D_C:], ng_ref[...], nb_ref[...])
    row = lax.broadcasted_iota(jnp.int32, (CHUNK_C, CHUNK_C), 0)
    col = lax.broadcasted_iota(jnp.int32, (CHUNK_C, CHUNK_C), 1)
    vb = vv.astype(BF16)
    for h in range(H_C):
        ws = jnp.where(row >= col, ws_ref[h], 0.0).astype(BF16)
        hs = slice(h * HEAD_C, (h + 1) * HEAD_C)
        for c in range(TL_C // CHUNK_C):
            rs = slice(c * CHUNK_C, (c + 1) * CHUNK_C)
            mixed = _dot(ws, vb[rs, hs]) + bst_ref[:, h:h + 1]
            yc_ref[rs, hs] = (u[rs, hs] * mixed).astype(BF16)

    @pl.when(l == pl.num_programs(1) - 1)
    def _():
        cv_ref[0] = vv[TL_C - CHUNK_C:, :]


def _gmlp_prompt(p, ng, nb, ws, bst):
    nl = SEQ // TL_C
    return pl.pallas_call(
        _gmlp_prompt_body,
        grid=(BATCH, nl),
        in_specs=[
            pl.BlockSpec((TL_C, COL_BLK), lambda bi, l: (bi * nl + l, UV_COL_BLK)),
            pl.BlockSpec((1, D_C), lambda bi, l: (0, 0)),
            pl.BlockSpec((1, D_C), lambda bi, l: (0, 0)),
            pl.BlockSpec((H_C, CHUNK_C, CHUNK_C), lambda bi, l: (0, 0, 0)),
            pl.BlockSpec((CHUNK_C, H_C), lambda bi, l: (0, 0)),
        ],
        out_specs=[
            pl.BlockSpec((TL_C, D_C), lambda bi, l: (bi * nl + l, 0)),
            pl.BlockSpec((1, CHUNK_C, D_C), lambda bi, l: (bi, 0, 0)),
        ],
        out_shape=[
            jax.ShapeDtypeStruct((T_PROMPT, D_C), BF16),
            jax.ShapeDtypeStruct((BATCH, CHUNK_C, D_C), F32),
        ],
        compiler_params=_cparams(("parallel", "arbitrary")),
        name="gmlp_prompt",
    )(p, ng, nb, ws, bst)


def _gmlp_sample_body(puv_ref, ng_ref, nb_ref, wsm_ref, bsm_ref, yc_ref, cv_ref):
    u, vv = [], []
    for t in range(DEC_SEQ):
        uv = _gelu(puv_ref[t])
        u.append(uv[:, :D_C])
        vt = _layernorm(uv[:, D_C:], ng_ref[...], nb_ref[...])
        vv.append(vt)
        cv_ref[t] = vt
    for i in range(DEC_SEQ):
        mixed = jnp.broadcast_to(bsm_ref[i:i + 1, :], (SB, D_C))
        for j in range(i + 1):
            mixed = mixed + wsm_ref[i * DEC_SEQ + j:i * DEC_SEQ + j + 1, :] * vv[j]
        yc_ref[i] = (u[i] * mixed).astype(BF16)


def _gmlp_sample(ps, ng, nb, wsm, bsm):
    return pl.pallas_call(
        _gmlp_sample_body,
        grid=(DEC_BATCH // SB,),
        in_specs=[
            pl.BlockSpec((DEC_SEQ, SB, COL_BLK), lambda s: (0, s, UV_COL_BLK)),
            pl.BlockSpec((1, D_C), lambda s: (0, 0)),
            pl.BlockSpec((1, D_C), lambda s: (0, 0)),
            pl.BlockSpec((DEC_SEQ * DEC_SEQ, D_C), lambda s: (0, 0)),
            pl.BlockSpec((SUBLANES, D_C), lambda s: (0, 0)),
        ],
        out_specs=[
            pl.BlockSpec((DEC_SEQ, SB, D_C), lambda s: (0, s, 0)),
            pl.BlockSpec((DEC_SEQ, SB, D_C), lambda s: (0, s, 0)),
        ],
        out_shape=[
            jax.ShapeDtypeStruct((DEC_SEQ, DEC_BATCH, D_C), BF16),
            jax.ShapeDtypeStruct((DEC_SEQ, DEC_BATCH, D_C), F32),
        ],
        compiler_params=_cparams(("parallel",)),
        name="gmlp_sample",
    )(ps, ng, nb, wsm, bsm)


def _softplus(x):
    return jnp.maximum(x, 0.0) + jnp.log1p(jnp.exp(-jnp.abs(x)))


def _l2norm(x):
    return x * lax.rsqrt(jnp.sum(x * x, axis=-1, keepdims=True) + EPS)


def _bdot(a, b):
    return _dot(a.astype(BF16), b.astype(BF16))


def _unit_lower_inverses(lmats, row, col):
    n = lmats[0].shape[0]
    eye = jnp.where(row == col, 1.0, 0.0).astype(F32)
    blk8 = (row >> 3) == (col >> 3)
    negs = [jnp.where(blk8, -l, 0.0) for l in lmats]
    ts = [eye + m for m in negs]
    p2 = [_bdot(m, m) for m in negs]
    ts = [t + _bdot(t, p) for t, p in zip(ts, p2)]
    p4 = [_bdot(p, p) for p in p2]
    ts = [t + _bdot(t, p) for t, p in zip(ts, p4)]
    sh = 3
    while (1 << sh) < n:
        sel = ((row >> (sh + 1)) == (col >> (sh + 1))) & ((row >> sh) != (col >> sh))
        offs = [jnp.where(sel, l, 0.0) for l in lmats]
        mid = [_bdot(t, o) for t, o in zip(ts, offs)]
        ts = [t - _bdot(m, t) for t, m in zip(ts, mid)]
        sh += 1
    return ts


def _delta_prompt_body(q_ref, k_ref, v_ref, z_ref, db_ref, cw_ref, alog_ref, dtb_ref, og_ref,
                       yb_ref, cb_ref, so_ref,
                       ext_ref, s_ref, qs_ref, ks_ref, vs_ref, gc_ref, gr_ref, bc_ref, os_ref):
    c = pl.program_id(1)
    n = CH_B
    hist = SUBLANES

    @pl.when(c == 0)
    def _():
        ext_ref[0:hist, :] = jnp.zeros((hist, 3 * D_B), F32)
        s_ref[...] = jnp.zeros((H_B, DK, DV), F32)

    ext_ref[hist:hist + n, 0:D_B] = q_ref[...]
    ext_ref[hist:hist + n, D_B:2 * D_B] = k_ref[...]
    ext_ref[hist:hist + n, 2 * D_B:] = v_ref[...]

    @pl.when(c == pl.num_programs(1) - 1)
    def _():
        cb_ref[0] = ext_ref[hist + n - (CONV_B_W - 1):hist + n, :]

    off = hist - (CONV_B_W - 1)
    acc = ext_ref[off:off + n, :] * cw_ref[0:1, :]
    for w in range(1, CONV_B_W):
        acc = acc + ext_ref[off + w:off + w + n, :] * cw_ref[w:w + 1, :]
    qkv = _silu(acc)
    ext_ref[0:hist, :] = ext_ref[n:n + hist, :]

    row = lax.broadcasted_iota(jnp.int32, (n, n), 0)
    col = lax.broadcasted_iota(jnp.int32, (n, n), 1)
    incl = row >= col

    db = db_ref[:, 0:LANES]
    g = -jnp.exp(alog_ref[...]) * _softplus(db + dtb_ref[...])
    beta = _sigmoid(db)
    gcs = _dot(jnp.where(incl, 1.0, 0.0).astype(F32), g, HIGHEST)
    gcs_t = gcs.T
    for h in range(H_B):
        hs = slice(h * DK, (h + 1) * DK)
        qs_ref[h] = _l2norm(qkv[:, hs]) * (DK ** -0.5)
        ks_ref[h] = _l2norm(qkv[:, D_B + h * DK:D_B + (h + 1) * DK])
        vs_ref[h] = qkv[:, 2 * D_B + h * DV:2 * D_B + (h + 1) * DV]
        gc_ref[h] = jnp.broadcast_to(gcs[:, h:h + 1], (n, n))
        gr_ref[h] = jnp.broadcast_to(gcs_t[h:h + 1, :], (n, n))
        bc_ref[h] = jnp.broadcast_to(beta[:, H_B + h:H_B + h + 1], (n, n))

    strict = row > col

    def head_group(grp, carry):
        hh = [grp * HEAD_GROUP + j for j in range(HEAD_GROUP)]
        q = [qs_ref[h] for h in hh]
        k = [ks_ref[h] for h in hh]
        gcol = [gc_ref[h] for h in hh]
        bcol = [bc_ref[h] for h in hh]
        decay = [jnp.where(incl, jnp.exp(jnp.where(incl, gc - gr_ref[h], 0.0)), 0.0)
                 for gc, h in zip(gcol, hh)]
        gam = [jnp.exp(gc) for gc in gcol]
        glast = [gc[n - 1:n, :] for gc in gcol]
        qkk = [_dot_nt(jnp.concatenate([qi, ki], axis=0).astype(BF16), ki.astype(BF16))
               for qi, ki in zip(q, k)]
        lmat = [jnp.where(strict, b * x[n:] * d, 0.0) for b, x, d in zip(bcol, qkk, decay)]
        tinv = _unit_lower_inverses(lmat, row, col)
        sol = [_bdot(t, jnp.concatenate([b * vs_ref[h], (b * gm) * ki], axis=1))
               for t, b, h, gm, ki in zip(tinv, bcol, hh, gam, k)]
        for j, h in enumerate(hh):
            s = s_ref[h]
            r = _bdot(jnp.concatenate([sol[j][:, DV:], q[j] * gam[j]], axis=0), s)
            ub = (sol[j][:, :DV] - r[:n]).astype(BF16)
            o = r[n:] + _dot((qkk[j][:n] * decay[j]).astype(BF16), ub)
            kd = k[j] * jnp.exp(glast[j] - gcol[j])
            s_ref[h] = jnp.exp(glast[j]) * s + _dot_tn(kd.astype(BF16), ub)
            os_ref[h] = _rms(o, og_ref[...])
        return carry

    lax.fori_loop(0, H_B // HEAD_GROUP, head_group, 0)

    for h in range(H_B):
        hs = slice(h * DV, (h + 1) * DV)
        yb_ref[:, hs] = (os_ref[h] * _silu(z_ref[:, hs])).astype(BF16)

    @pl.when(c == pl.num_programs(1) - 1)
    def _():
        so_ref[0] = s_ref[...]


def _delta_prompt(p, cw, alog, dtb, og):
    nc = SEQ // CH_B
    tile = pltpu.VMEM((H_B, CH_B, CH_B), F32)
    return pl.pallas_call(
        _delta_prompt_body,
        grid=(BATCH, nc),
        in_specs=[
            pl.BlockSpec((CH_B, COL_BLK), lambda bi, c: (bi * nc + c, 1)),
            pl.BlockSpec((CH_B, COL_BLK), lambda bi, c: (bi * nc + c, 2)),
            pl.BlockSpec((CH_B, COL_BLK), lambda bi, c: (bi * nc + c, 3)),
            pl.BlockSpec((CH_B, COL_BLK), lambda bi, c: (bi * nc + c, 4)),
            pl.BlockSpec((CH_B, SMALL_BLK), lambda bi, c: (bi * nc + c, DB_COL_BLK)),
            pl.BlockSpec((CONV_B_W, 3 * D_B), lambda bi, c: (0, 0)),
            pl.BlockSpec((1, LANES), lambda bi, c: (0, 0)),
            pl.BlockSpec((1, LANES), lambda bi, c: (0, 0)),
            pl.BlockSpec((1, DV), lambda bi, c: (0, 0)),
        ],
        out_specs=[
            pl.BlockSpec((CH_B, D_B), lambda bi, c: (bi * nc + c, 0)),
            pl.BlockSpec((1, CONV_B_W - 1, 3 * D_B), lambda bi, c: (bi, 0, 0)),
            pl.BlockSpec((1, H_B, DK, DV), lambda bi, c: (bi, 0, 0, 0)),
        ],
        out_shape=[
            jax.ShapeDtypeStruct((T_PROMPT, D_B), BF16),
            jax.ShapeDtypeStruct((BATCH, CONV_B_W - 1, 3 * D_B), F32),
            jax.ShapeDtypeStruct((BATCH, H_B, DK, DV), F32),
        ],
        scratch_shapes=[
            pltpu.VMEM((SUBLANES + CH_B, 3 * D_B), F32),
            pltpu.VMEM((H_B, DK, DV), F32),
            tile, tile, tile, tile, tile, tile, tile,
        ],
        compiler_params=_cparams(("parallel", "arbitrary")),
        name="delta_prompt",
    )(p, p, p, p, p, cw, alog, dtb, og)


def _delta_sample_body(q_ref, k_ref, v_ref, z_ref, db_ref, st_ref, s0_ref, cw_ref, alog_ref,
                       dtb_ref, og_ref, all_layers_hbm, yb_ref, cb_ref, so_ref):
    del all_layers_hbm
    ns = CONV_B_W - 1
    nt = DEC_SEQ

    def ext_part(r, lo, ref):
        return st_ref[r, :, lo:lo + D_B] if r < ns else ref[r - ns]

    def conv(lo, ref):
        out = []
        for t in range(nt):
            acc = ext_part(t, lo, ref) * cw_ref[0:1, lo:lo + D_B]
            for w in range(1, CONV_B_W):
                acc = acc + ext_part(t + w, lo, ref) * cw_ref[w:w + 1, lo:lo + D_B]
            out.append(_silu(acc))
        return out

    qc, kc, vc = conv(0, q_ref), conv(D_B, k_ref), conv(2 * D_B, v_ref)
    for r in range(ns):
        for lo, ref in ((0, q_ref), (D_B, k_ref), (2 * D_B, v_ref)):
            cb_ref[r, :, lo:lo + D_B] = ext_part(r + nt, lo, ref)

    g, beta = [], []
    for t in range(nt):
        db = db_ref[t][:, 0:LANES]
        g.append(-jnp.exp(alog_ref[...]) * _softplus(db + dtb_ref[...]))
        beta.append(_sigmoid(db))
    gc = [g[0]]
    for t in range(1, nt):
        gc.append(gc[t - 1] + g[t])

    rowid = lax.broadcasted_iota(jnp.int32, (SB, DV), 0)
    rows4 = lax.broadcasted_iota(jnp.int32, (nt * SB, DV), 0) & (SB - 1)

    for h in range(H_B):
        hs = slice(h * DK, (h + 1) * DK)
        q = [_l2norm(qc[t][:, hs]) * (DK ** -0.5) for t in range(nt)]
        k = [_l2norm(kc[t][:, hs]) for t in range(nt)]
        v = [vc[t][:, hs] for t in range(nt)]
        gch = [gc[t][:, h:h + 1] for t in range(nt)]
        bh = [beta[t][:, H_B + h:H_B + h + 1] for t in range(nt)]
        gam = [jnp.exp(x) for x in gch]

        lhs = jnp.concatenate(k + [q[t] * gam[t] for t in range(nt)], axis=0).astype(BF16)
        ks0 = [jnp.zeros((SB, DV), F32) for _ in range(nt)]
        qs0 = [jnp.zeros((SB, DV), F32) for _ in range(nt)]
        for s in range(SB):
            res = _dot(lhs, s0_ref[s, h].astype(BF16))
            for t in range(nt):
                ks0[t] = jnp.where(rowid == s, res[t * SB:(t + 1) * SB], ks0[t])
                qs0[t] = jnp.where(rowid == s, res[(nt + t) * SB:(nt + t + 1) * SB], qs0[t])

        u = []
        for t in range(nt):
            acc = bh[t] * (v[t] - gam[t] * ks0[t])
            for j in range(t):
                kk = jnp.sum(k[t] * k[j], axis=-1, keepdims=True)
                acc = acc - (bh[t] * kk * jnp.exp(gch[t] - gch[j])) * u[j]
            u.append(acc)

        for t in range(nt):
            o = qs0[t]
            for j in range(t + 1):
                qk = jnp.sum(q[t] * k[j], axis=-1, keepdims=True)
                o = o + (qk * jnp.exp(gch[t] - gch[j])) * u[j]
            yb_ref[t, :, hs] = (_rms(o, og_ref[...]) * _silu(z_ref[t][:, hs])).astype(BF16)

        kd = jnp.concatenate([k[t] * jnp.exp(gch[nt - 1] - gch[t]) for t in range(nt)],
                             axis=0).astype(BF16)
        uall = jnp.concatenate(u, axis=0)
        for s in range(SB):
            um = jnp.where(rows4 == s, uall, 0.0).astype(BF16)
            gl = jnp.exp(gch[nt - 1][s:s + 1, :])
            so_ref[s, h] = gl * s0_ref[s, h] + _dot_tn(kd, um)


def _delta_sample(ps, conv_state_t, delta_state, new_state, layer, cw, alog, dtb, og):
    ns = CONV_B_W - 1
    n_before = 11
    return pl.pallas_call(
        _delta_sample_body,
        grid=(DEC_BATCH // SB,),
        input_output_aliases={n_before: 2},
        in_specs=[
            pl.BlockSpec((DEC_SEQ, SB, COL_BLK), lambda s: (0, s, 1)),
            pl.BlockSpec((DEC_SEQ, SB, COL_BLK), lambda s: (0, s, 2)),
            pl.BlockSpec((DEC_SEQ, SB, COL_BLK), lambda s: (0, s, 3)),
            pl.BlockSpec((DEC_SEQ, SB, COL_BLK), lambda s: (0, s, 4)),
            pl.BlockSpec((DEC_SEQ, SB, SMALL_BLK), lambda s: (0, s, DB_COL_BLK)),
            pl.BlockSpec((ns, SB, 3 * D_B), lambda s: (0, s, 0)),
            pl.BlockSpec((SB, None, H_B, DK, DV), lambda s: (s, layer, 0, 0, 0)),
            pl.BlockSpec((CONV_B_W, 3 * D_B), lambda s: (0, 0)),
            pl.BlockSpec((1, LANES), lambda s: (0, 0)),
            pl.BlockSpec((1, LANES), lambda s: (0, 0)),
            pl.BlockSpec((1, DV), lambda s: (0, 0)),
            pl.BlockSpec(memory_space=pl.ANY),
        ],
        out_specs=[
            pl.BlockSpec((DEC_SEQ, SB, D_B), lambda s: (0, s, 0)),
            pl.BlockSpec((ns, SB, 3 * D_B), lambda s: (0, s, 0)),
            pl.BlockSpec((SB, None, H_B, DK, DV), lambda s: (s, layer, 0, 0, 0)),
        ],
        out_shape=[
            jax.ShapeDtypeStruct((DEC_SEQ, DEC_BATCH, D_B), BF16),
            jax.ShapeDtypeStruct((ns, DEC_BATCH, 3 * D_B), F32),
            jax.ShapeDtypeStruct((DEC_BATCH, DEPTH, H_B, DK, DV), F32),
        ],
        compiler_params=_cparams(("parallel",)),
        name="delta_sample",
    )(ps, ps, ps, ps, ps, conv_state_t, delta_state, cw, alog, dtb, og, new_state)


def _row(x):
    return x.reshape(1, -1)


def _pad_lanes(x):
    return jnp.pad(x, (0, LANES - x.shape[0])).reshape(1, LANES)


def kernel(x_prompt, x_sample, state_conv_a, state_conv_b, state_delta, norm_ffn1, ffn1_gate,
           ffn1_up, ffn1_down, norm_mix, w_in, conv_a_w, conv_a_b, norm_a_g, norm_a_b, conv_b_w,
           a_log, dt_bias, norm_o_g, norm_c_g, norm_c_b, sgu_w, sgu_b, w_out, norm_ffn2,
           ffn2_gate, ffn2_up, ffn2_down, norm_final):
    wg1, wu1, wd1 = ffn1_gate.astype(BF16), ffn1_up.astype(BF16), ffn1_down.astype(BF16)
    wg2, wu2, wd2 = ffn2_gate.astype(BF16), ffn2_up.astype(BF16), ffn2_down.astype(BF16)
    w_in_b = w_in.astype(BF16)
    w_uv_b = w_in[:, :, N_IN - 2 * D_C:].astype(BF16)
    w_out_b = w_out.astype(BF16)

    x = [x_prompt.reshape(T_PROMPT, D_MODEL),
         jnp.transpose(x_sample, (1, 0, 2)).reshape(T_SAMPLE, D_MODEL)]
    gfin = _row(norm_final)
    s_s = jnp.zeros((DEC_BATCH, DEPTH, H_B, DK, DV), F32)

    wsm = jnp.repeat(sgu_w[:, :, :DEC_SEQ, :DEC_SEQ].reshape(DEPTH, H_C, DEC_SEQ * DEC_SEQ),
                     HEAD_C, axis=1)
    wsm = jnp.transpose(wsm, (0, 2, 1))
    bsm = jnp.transpose(jnp.repeat(sgu_b[:, :, :SUBLANES], HEAD_C, axis=1), (0, 2, 1))
    bst = jnp.transpose(sgu_b, (0, 2, 1))

    outs = {k: [] for k in ("ap", "bp", "sp", "vp", "as", "bs", "ss", "vs")}
    for i in range(DEPTH):
        x = _ffn(x, _row(norm_ffn1[i]), wg1, wu1, wd1, gfin,
                 layer=i, split_out=False, final_norm=False)[0]
        p = _proj_in(x, _row(norm_mix[i]), w_in_b, w_uv_b, i)
        ps = p[T_PROMPT:].reshape(DEC_SEQ, DEC_BATCH, N_PROJ)

        caw, cab = conv_a_w[i], _row(conv_a_b[i])
        nag, nab = _row(norm_a_g[i]), _row(norm_a_b[i])
        ya_p, a_p = _conv_a_prompt(p, caw, cab, nag, nab)
        ya_s, a_s = _conv_a_sample(ps, jnp.transpose(state_conv_a[:, i], (1, 0, 2)),
                                   caw, cab, nag, nab)
        a_s = jnp.transpose(a_s, (1, 0, 2))

        alog, dtb, og = _pad_lanes(a_log[i]), _pad_lanes(dt_bias[i]), _row(norm_o_g[i])
        yb_p, b_p, s_p = _delta_prompt(p, conv_b_w[i], alog, dtb, og)
        yb_s, b_s, s_s = _delta_sample(ps, jnp.transpose(state_conv_b[:, i], (1, 0, 2)),
                                       state_delta, s_s, i, conv_b_w[i], alog, dtb, og)
        b_s = jnp.transpose(b_s, (1, 0, 2))

        ncg, ncb = _row(norm_c_g[i]), _row(norm_c_b[i])
        yc_p, v_p = _gmlp_prompt(p, ncg, ncb, sgu_w[i], bst[i])
        yc_s, v_s = _gmlp_sample(ps, ncg, ncb, wsm[i], bsm[i])

        y_s = [y.reshape(T_SAMPLE, y.shape[-1]) for y in (ya_s, yb_s, yc_s)]
        x = _proj_out(x, [ya_p, yb_p, yc_p], y_s, w_out_b, i)
        last = i == DEPTH - 1
        x = _ffn([x], _row(norm_ffn2[i]), wg2, wu2, wd2, gfin,
                 layer=i, split_out=last, final_norm=last)

        outs["ap"].append(a_p); outs["bp"].append(b_p); outs["sp"].append(s_p)
        outs["vp"].append(v_p); outs["as"].append(a_s); outs["bs"].append(b_s)
        outs["vs"].append(jnp.transpose(v_s, (1, 0, 2)))

    y_prompt = x[0].reshape(BATCH, SEQ, D_MODEL)
    y_sample = jnp.transpose(x[1].reshape(DEC_SEQ, DEC_BATCH, D_MODEL), (1, 0, 2))
    st = lambda k: jnp.stack(outs[k], axis=1)
    return (y_prompt, y_sample, st("ap"), st("bp"), st("sp"), st("vp"),
            st("as"), st("bs"), s_s, st("vs"))
```

```python
import functools

import numpy as np
import jax
import jax.numpy as jnp
from jax import lax
from jax.experimental import pallas as pl
from jax.experimental.pallas import tpu as pltpu

F32 = jnp.float32
BF16 = jnp.bfloat16
HIGHEST = lax.Precision.HIGHEST

D_MODEL = 4096
BATCH = 4
SEQ = 2048
DEPTH = 2
DEC_BATCH = 128
DEC_SEQ = 4
D_A = 1024
D_B = 2048
D_C = 1024
DK = 128
DV = 128
H_B = 16
H_C = 8
HEAD_C = 128
CONV_A_W = 31
CONV_B_W = 4
CHUNK_C = 128
D_FF = 11008
EPS = 1e-6

T_PROMPT = BATCH * SEQ
T_SAMPLE = DEC_BATCH * DEC_SEQ
T_ALL = T_PROMPT + T_SAMPLE

N_IN = 2 * D_A + 4 * D_B + 2 * H_B + 2 * D_C
N_MAIN = 2 * D_A + 4 * D_B
COL_BLK = 2048
SMALL_BLK = 512
TN_IN = 512
N_PROJ = N_MAIN + 2 * D_C + TN_IN
UV_COL_BLK = N_MAIN // COL_BLK
DB_COL_BLK = (N_MAIN + 2 * D_C) // SMALL_BLK
LANES = 128
SUBLANES = 8

VMEM_LIMIT = 56 * 1024 * 1024

TM = 512
DOWN_COLS = 1024
NORM_ROWS_FFN = 128
TM_IN = 1088
NORM_ROWS = 272
TF = 512
TN_OUT = 1024
TL_A = 256
TL_C = 256
CH_B = 128
HEAD_GROUP = 16
SB = 8


def _cparams(sem):
    return pltpu.CompilerParams(dimension_semantics=sem, vmem_limit_bytes=VMEM_LIMIT)


def _sigmoid(x):
    return jax.nn.sigmoid(x)


def _silu(x):
    return x * _sigmoid(x)


def _rms(x, g):
    return x * lax.rsqrt(jnp.mean(x * x, axis=-1, keepdims=True) + EPS) * g


def _layernorm(x, g, b):
    mu = jnp.mean(x, axis=-1, keepdims=True)
    xc = x - mu
    var = jnp.mean(xc * xc, axis=-1, keepdims=True)
    return xc * lax.rsqrt(var + EPS) * g + b


def _dot(a, b, precision=None):
    return jnp.dot(a, b, preferred_element_type=F32, precision=precision)


def _dot_nt(a, b, precision=None):
    return lax.dot_general(a, b, (((1,), (1,)), ((), ())), preferred_element_type=F32,
                           precision=precision)


def _dot_tn(a, b, precision=None):
    return lax.dot_general(a, b, (((0,), (0,)), ((), ())), preferred_element_type=F32,
                           precision=precision)


N_PROMPT_TILES = T_PROMPT // TM
N_TILES = T_ALL // TM


def _ffn_body(*refs, split_in, split_out, layer, final_norm):
    n_in = 2 if split_in else 1
    n_out = 2 if split_out else 1
    x_hbm = refs[:n_in]
    g_ref, wg_hbm, wu_hbm, wd_hbm, gf_ref = refs[n_in:n_in + 5]
    o_hbm = refs[n_in + 5:n_in + 5 + n_out]
    (acc_ref, h_ref, wg_buf, wu_buf, wd_buf, ld_sem, wb_sem,
     w_sem) = refs[n_in + 5 + n_out:]
    i = pl.program_id(0)
    ni = pl.num_programs(0)
    slot = i % 2

    def tile_dma(tile, sl, arrays, sem, to_vmem, op):
        def run(hbm, row0):
            rows = hbm.at[pl.ds(pl.multiple_of(row0, TM), TM)]
            src, dst = (rows, acc_ref.at[sl]) if to_vmem else (acc_ref.at[sl], rows)
            cp = pltpu.make_async_copy(src, dst, sem.at[sl])
            cp.start() if op == "start" else cp.wait()

        if len(arrays) == 2:
            @pl.when(tile < N_PROMPT_TILES)
            def _():
                run(arrays[0], tile * TM)

            @pl.when(tile >= N_PROMPT_TILES)
            def _():
                run(arrays[1], (tile - N_PROMPT_TILES) * TM)
        else:
            run(arrays[0], tile * TM)

    load = functools.partial(tile_dma, arrays=x_hbm, sem=ld_sem, to_vmem=True)
    writeback = functools.partial(tile_dma, arrays=o_hbm, sem=wb_sem, to_vmem=False)

    d_ff = wg_hbm.shape[2]
    n_wide = d_ff // TF
    tail = d_ff - n_wide * TF
    n_steps = n_wide + 1

    def weight_dma(jt, ws, op):
        def run(c0, width):
            copies = (
                (wg_hbm.at[layer, :, pl.ds(c0, width)], wg_buf.at[ws, :, pl.ds(0, width)], 0),
                (wu_hbm.at[layer, :, pl.ds(c0, width)], wu_buf.at[ws, :, pl.ds(0, width)], 1),
                (wd_hbm.at[layer, pl.ds(c0, width), :], wd_buf.at[ws, pl.ds(0, width), :], 2),
            )
            for src, dst, which in copies:
                cp = pltpu.make_async_copy(src, dst, w_sem.at[which, ws])
                cp.start() if op == "start" else cp.wait()

        if isinstance(jt, int):
            run(jt * TF, TF if jt < n_wide else tail)
        else:
            @pl.when(jt < n_wide)
            def _():
                run(pl.multiple_of(jt * TF, TF), TF)

            @pl.when(jt == n_wide)
            def _():
                run(n_wide * TF, tail)

    def hidden_tile(ws, width):
        h = h_ref[...]
        hg = _dot(h, wg_buf[ws, :, 0:width])
        hu = _dot(h, wu_buf[ws, :, 0:width])
        a = (0.5 * _silu(hg) * hu).astype(BF16)
        for c0 in range(0, D_MODEL, DOWN_COLS):
            acc_ref[slot, :, c0:c0 + DOWN_COLS] += _dot(a, wd_buf[ws, 0:width, c0:c0 + DOWN_COLS])

    @pl.when(i == 0)
    def _():
        load(i, slot, op="start")
        weight_dma(0, 0, "start")

    load(i, slot, op="wait")
    def by_row_chunks(fn):
        def chunk(c, carry):
            fn(pl.ds(pl.multiple_of(c * NORM_ROWS_FFN, NORM_ROWS_FFN), NORM_ROWS_FFN))
            return carry

        lax.fori_loop(0, TM // NORM_ROWS_FFN, chunk, 0)

    def set_h(rows):
        h_ref[rows, :] = _rms(acc_ref[slot, rows, :], g_ref[...]).astype(BF16)

    by_row_chunks(set_h)

    def step(j, carry):
        ws = (i * n_steps + j) % 2
        weight_dma(j, ws, "wait")
        weight_dma(j + 1, 1 - ws, "start")

        @pl.when((j == 1) & (i + 1 < ni))
        def _():
            @pl.when(i >= 1)
            def _():
                writeback(i - 1, 1 - slot, op="wait")

            load(i + 1, 1 - slot, op="start")

        hidden_tile(ws, TF)
        return carry

    lax.fori_loop(0, n_wide, step, 0)

    ws_tail = (i * n_steps + n_wide) % 2
    weight_dma(n_wide, ws_tail, "wait")

    @pl.when(i + 1 < ni)
    def _():
        weight_dma(0, 1 - ws_tail, "start")

    hidden_tile(ws_tail, tail)

    if final_norm:
        def set_final(rows):
            acc_ref[slot, rows, :] = _rms(acc_ref[slot, rows, :], gf_ref[...])

        by_row_chunks(set_final)
    writeback(i, slot, op="start")

    @pl.when(i == ni - 1)
    def _():
        writeback(i - 1, 1 - slot, op="wait")
        writeback(i, slot, op="wait")


def _ffn(xs, g, wg, wu, wd, gf, *, layer, split_out, final_norm):
    tail = wg.shape[2] % TF
    assert N_TILES >= 2 and wg.shape[2] // TF >= 2 and tail > 0 and tail % LANES == 0
    split_in = len(xs) == 2
    if split_out:
        out_shape = [jax.ShapeDtypeStruct((T_PROMPT, D_MODEL), F32),
                     jax.ShapeDtypeStruct((T_SAMPLE, D_MODEL), F32)]
    else:
        out_shape = [jax.ShapeDtypeStruct((T_ALL, D_MODEL), F32)]
    hbm = pl.BlockSpec(memory_space=pl.ANY)
    vec = pl.BlockSpec((1, D_MODEL), lambda i: (0, 0))
    return pl.pallas_call(
        functools.partial(_ffn_body, split_in=split_in, split_out=split_out, layer=layer,
                          final_norm=final_norm),
        grid=(N_TILES,),
        in_specs=[hbm] * len(xs) + [vec, hbm, hbm, hbm, vec],
        out_specs=[hbm] * len(out_shape),
        out_shape=out_shape,
        scratch_shapes=[
            pltpu.VMEM((2, TM, D_MODEL), F32),
            pltpu.VMEM((TM, D_MODEL), BF16),
            pltpu.VMEM((2, D_MODEL, TF), BF16),
            pltpu.VMEM((2, D_MODEL, TF), BF16),
            pltpu.VMEM((2, TF, D_MODEL), BF16),
            pltpu.SemaphoreType.DMA((2,)),
            pltpu.SemaphoreType.DMA((2,)),
            pltpu.SemaphoreType.DMA((3, 2)),
        ],
        compiler_params=_cparams(("arbitrary",)),
        name="ffn_final" if final_norm else "ffn",
    )(*xs, g, wg, wu, wd, gf)


N_MAIN_TILES = N_MAIN // TN_IN
N_UV_TILES = 2 * D_C // TN_IN


def _proj_in_body(x_ref, g_ref, w_ref, wuv_ref, o_ref, h_ref):
    j = pl.program_id(1)

    @pl.when(j == 0)
    def _():
        def chunk(c, carry):
            rows = pl.ds(pl.multiple_of(c * NORM_ROWS, NORM_ROWS), NORM_ROWS)
            h_ref[rows, :] = _rms(x_ref[rows, :], g_ref[...]).astype(BF16)
            return carry

        lax.fori_loop(0, TM_IN // NORM_ROWS, chunk, 0)

    is_uv =(j >= N_MAIN_TILES) & (j < N_MAIN_TILES + N_UV_TILES)

    @pl.when(is_uv)
    def _():
        o_ref[...] = _dot(h_ref[...], wuv_ref[...])

    @pl.when(jnp.logical_not(is_uv))
    def _():
        o_ref[...] = _dot(h_ref[...], w_ref[...])


def _proj_in(x, g, w, w_uv, layer):
    t = x.shape[0]

    def main_tile(i, j):
        return (layer, 0, jnp.minimum(j, N_MAIN_TILES))

    def uv_tile(i, j):
        return (layer, 0, jnp.clip(j - N_MAIN_TILES, 0, N_UV_TILES - 1))

    return pl.pallas_call(
        _proj_in_body,
        grid=(t // TM_IN, N_PROJ // TN_IN),
        in_specs=[
            pl.BlockSpec((TM_IN, D_MODEL), lambda i, j: (i, 0), pipeline_mode=pl.Buffered(1)),
            pl.BlockSpec((1, D_MODEL), lambda i, j: (0, 0)),
            pl.BlockSpec((None, D_MODEL, TN_IN), main_tile),
            pl.BlockSpec((None, D_MODEL, TN_IN), uv_tile),
        ],
        out_specs=pl.BlockSpec((TM_IN, TN_IN), lambda i, j: (i, j)),
        out_shape=jax.ShapeDtypeStruct((t, N_PROJ), F32),
        scratch_shapes=[pltpu.VMEM((TM_IN, D_MODEL), BF16)],
        compiler_params=_cparams(("parallel", "arbitrary")),
        name="proj_in",
    )(x, g, w, w_uv)


def _proj_out_body(x_ref, yap_ref, ybp_ref, ycp_ref, yas_ref, ybs_ref, ycs_ref, w_ref, o_ref):
    def mix(ya_ref, yb_ref, yc_ref):
        acc = x_ref[...]
        acc += _dot(ya_ref[...], w_ref[0:D_A, :])
        acc += _dot(yb_ref[...], w_ref[D_A:D_A + D_B, :])
        acc += _dot(yc_ref[...], w_ref[D_A + D_B:, :])
        o_ref[...] = acc

    is_prompt = pl.program_id(0) < N_PROMPT_TILES

    @pl.when(is_prompt)
    def _():
        mix(yap_ref, ybp_ref, ycp_ref)

    @pl.when(jnp.logical_not(is_prompt))
    def _():
        mix(yas_ref, ybs_ref, ycs_ref)


def _proj_out(x, y_prompt, y_sample, w, layer):
    assert T_SAMPLE == TM

    def prompt_spec(width):
        return pl.BlockSpec((TM, width), lambda i, j: (jnp.minimum(i, N_PROMPT_TILES - 1), 0))

    def sample_spec(width):
        return pl.BlockSpec((TM, width), lambda i, j: (0, 0))

    widths = (D_A, D_B, D_C)
    return pl.pallas_call(
        _proj_out_body,
        grid=(N_TILES, D_MODEL // TN_OUT),
        in_specs=[pl.BlockSpec((TM, TN_OUT), lambda i, j: (i, j))]
        + [prompt_spec(wd) for wd in widths] + [sample_spec(wd) for wd in widths]
        + [pl.BlockSpec((None, D_MODEL, TN_OUT), lambda i, j: (layer, 0, j))],
        out_specs=pl.BlockSpec((TM, TN_OUT), lambda i, j: (i, j)),
        out_shape=jax.ShapeDtypeStruct((T_ALL, D_MODEL), F32),
        compiler_params=_cparams(("parallel", "arbitrary")),
        name="proj_out",
    )(x, *y_prompt, *y_sample, w)


HIST = 32


RB_A = 32


def _conv_a_prompt_body(pa_ref, w_ref, b_ref, lg_ref, lb_ref, ya_ref, st_ref,
                        ext_ref, sh_ref, conv_ref):
    l = pl.program_id(1)

    @pl.when(l == 0)
    def _():
        ext_ref[0:HIST, :] = jnp.zeros((HIST, D_A), F32)

    pa = pa_ref[...]
    ext_ref[HIST:HIST + TL_A, :] = pa[:, :D_A] * _sigmoid(pa[:, D_A:])
    n_sh = HIST + TL_A - SUBLANES
    for r in range(1, SUBLANES):
        sh_ref[r - 1] = ext_ref[r:r + n_sh, :]
    off = HIST - (CONV_A_W - 1)

    def block(rb, carry):
        base = rb * RB_A
        acc = jnp.broadcast_to(b_ref[...], (RB_A, D_A))
        for w in range(CONV_A_W):
            r = (off + w) % SUBLANES
            rows = pl.ds(pl.multiple_of(base + (off + w - r), SUBLANES), RB_A)
            x = ext_ref[rows, :] if r == 0 else sh_ref[r - 1, rows, :]
            acc = acc + x * w_ref[w:w + 1, :]
        conv_ref[pl.ds(pl.multiple_of(base, RB_A), RB_A), :] = acc
        return carry

    lax.fori_loop(0, TL_A // RB_A, block, 0)
    ya_ref[...] = _silu(_layernorm(conv_ref[...], lg_ref[...], lb_ref[...])).astype(BF16)

    @pl.when(l == pl.num_programs(1) - 1)
    def _():
        st_ref[0] = ext_ref[HIST + TL_A - (CONV_A_W - 1):HIST + TL_A, :]

    ext_ref[0:HIST, :] = ext_ref[TL_A:TL_A + HIST, :]


def _conv_a_prompt(p, w, b, lg, lb):
    nl = SEQ // TL_A
    return pl.pallas_call(
        _conv_a_prompt_body,
        grid=(BATCH, nl),
        in_specs=[
            pl.BlockSpec((TL_A, COL_BLK), lambda bi, l: (bi * nl + l, 0)),
            pl.BlockSpec((CONV_A_W, D_A), lambda bi, l: (0, 0)),
            pl.BlockSpec((1, D_A), lambda bi, l: (0, 0)),
            pl.BlockSpec((1, D_A), lambda bi, l: (0, 0)),
            pl.BlockSpec((1, D_A), lambda bi, l: (0, 0)),
        ],
        out_specs=[
            pl.BlockSpec((TL_A, D_A), lambda bi, l: (bi * nl + l, 0)),
            pl.BlockSpec((1, CONV_A_W - 1, D_A), lambda bi, l: (bi, 0, 0)),
        ],
        out_shape=[
            jax.ShapeDtypeStruct((T_PROMPT, D_A), BF16),
            jax.ShapeDtypeStruct((BATCH, CONV_A_W - 1, D_A), F32),
        ],
        scratch_shapes=[
            pltpu.VMEM((HIST + TL_A, D_A), F32),
            pltpu.VMEM((SUBLANES - 1, HIST + TL_A - SUBLANES, D_A), F32),
            pltpu.VMEM((TL_A, D_A), F32),
        ],
        compiler_params=_cparams(("parallel", "arbitrary")),
        name="conv_a_prompt",
    )(p, w, b, lg, lb)


def _conv_a_sample_body(pa_ref, st_ref, w_ref, b_ref, lg_ref, lb_ref, ya_ref, ns_ref):
    ns = CONV_A_W - 1
    ga = []
    for t in range(DEC_SEQ):
        pa = pa_ref[t]
        ga.append(pa[:, :D_A] * _sigmoid(pa[:, D_A:]))

    def ext_row(r):
        return st_ref[r] if r < ns else ga[r - ns]

    for t in range(DEC_SEQ):
        acc = jnp.broadcast_to(b_ref[...], (SB, D_A))
        for w in range(CONV_A_W):
            acc = acc + ext_row(t + w) * w_ref[w:w + 1, :]
        ya_ref[t] = _silu(_layernorm(acc, lg_ref[...], lb_ref[...])).astype(BF16)
    for r in range(ns):
        ns_ref[r] = ext_row(r + DEC_SEQ)


def _conv_a_sample(ps, state_t, w, b, lg, lb):
    ns = CONV_A_W - 1
    return pl.pallas_call(
        _conv_a_sample_body,
        grid=(DEC_BATCH // SB,),
        in_specs=[
            pl.BlockSpec((DEC_SEQ, SB, COL_BLK), lambda s: (0, s, 0)),
            pl.BlockSpec((ns, SB, D_A), lambda s: (0, s, 0)),
            pl.BlockSpec((CONV_A_W, D_A), lambda s: (0, 0)),
            pl.BlockSpec((1, D_A), lambda s: (0, 0)),
            pl.BlockSpec((1, D_A), lambda s: (0, 0)),
            pl.BlockSpec((1, D_A), lambda s: (0, 0)),
        ],
        out_specs=[
            pl.BlockSpec((DEC_SEQ, SB, D_A), lambda s: (0, s, 0)),
            pl.BlockSpec((ns, SB, D_A), lambda s: (0, s, 0)),
        ],
        out_shape=[
            jax.ShapeDtypeStruct((DEC_SEQ, DEC_BATCH, D_A), BF16),
            jax.ShapeDtypeStruct((ns, DEC_BATCH, D_A), F32),
        ],
        compiler_params=_cparams(("parallel",)),
        name="conv_a_sample",
    )(ps, state_t, w, b, lg, lb)


_SQRT_HALF = float(np.sqrt(0.5))


def _gelu(x):
    return 0.5 * x * (1.0 + lax.erf(x * _SQRT_HALF))


def _gmlp_prompt_body(puv_ref, ng_ref, nb_ref, ws_ref, bst_ref, yc_ref, cv_ref):
    l = pl.program_id(1)
    uv = _gelu(puv_ref[...])
    u = uv[:, :D_C]
    vv = _layernorm(uv[:, D_C:], ng_ref[...], nb_ref[...])
    row = lax.broadcasted_iota(jnp.int32, (CHUNK_C, CHUNK_C), 0)
    col = lax.broadcasted_iota(jnp.int32, (CHUNK_C, CHUNK_C), 1)
    vb = vv.astype(BF16)
    for h in range(H_C):
        ws = jnp.where(row >= col, ws_ref[h], 0.0).astype(BF16)
        hs = slice(h * HEAD_C, (h + 1) * HEAD_C)
        for c in range(TL_C // CHUNK_C):
            rs = slice(c * CHUNK_C, (c + 1) * CHUNK_C)
            mixed = _dot(ws, vb[rs, hs]) + bst_ref[:, h:h + 1]
            yc_ref[rs, hs] = (u[rs, hs] * mixed).astype(BF16)

    @pl.when(l == pl.num_programs(1) - 1)
    def _():
        cv_ref[0] = vv[TL_C - CHUNK_C:, :]


def _gmlp_prompt(p, ng, nb, ws, bst):
    nl = SEQ // TL_C
    return pl.pallas_call(
        _gmlp_prompt_body,
        grid=(BATCH, nl),
        in_specs=[
            pl.BlockSpec((TL_C, COL_BLK), lambda bi, l: (bi * nl + l, UV_COL_BLK)),
            pl.BlockSpec((1, D_C), lambda bi, l: (0, 0)),
            pl.BlockSpec((1, D_C), lambda bi, l: (0, 0)),
            pl.BlockSpec((H_C, CHUNK_C, CHUNK_C), lambda bi, l: (0, 0, 0)),
            pl.BlockSpec((CHUNK_C, H_C), lambda bi, l: (0, 0)),
        ],
        out_specs=[
            pl.BlockSpec((TL_C, D_C), lambda bi, l: (bi * nl + l, 0)),
            pl.BlockSpec((1, CHUNK_C, D_C), lambda bi, l: (bi, 0, 0)),
        ],
        out_shape=[
            jax.ShapeDtypeStruct((T_PROMPT, D_C), BF16),
            jax.ShapeDtypeStruct((BATCH, CHUNK_C, D_C), F32),
        ],
        compiler_params=_cparams(("parallel", "arbitrary")),
        name="gmlp_prompt",
    )(p, ng, nb, ws, bst)


def _gmlp_sample_body(puv_ref, ng_ref, nb_ref, wsm_ref, bsm_ref, yc_ref, cv_ref):
    u, vv = [], []
    for t in range(DEC_SEQ):
        uv = _gelu(puv_ref[t])
        u.append(uv[:, :D_C])
        vt = _layernorm(uv[:, D_C:], ng_ref[...], nb_ref[...])
        vv.append(vt)
        cv_ref[t] = vt
    for i in range(DEC_SEQ):
        mixed = jnp.broadcast_to(bsm_ref[i:i + 1, :], (SB, D_C))
        for j in range(i + 1):
            mixed = mixed + wsm_ref[i * DEC_SEQ + j:i * DEC_SEQ + j + 1, :] * vv[j]
        yc_ref[i] = (u[i] * mixed).astype(BF16)


def _gmlp_sample(ps, ng, nb, wsm, bsm):
    return pl.pallas_call(
        _gmlp_sample_body,
        grid=(DEC_BATCH // SB,),
        in_specs=[
            pl.BlockSpec((DEC_SEQ, SB, COL_BLK), lambda s: (0, s, UV_COL_BLK)),
            pl.BlockSpec((1, D_C), lambda s: (0, 0)),
            pl.BlockSpec((1, D_C), lambda s: (0, 0)),
            pl.BlockSpec((DEC_SEQ * DEC_SEQ, D_C), lambda s: (0, 0)),
            pl.BlockSpec((SUBLANES, D_C), lambda s: (0, 0)),
        ],
        out_specs=[
            pl.BlockSpec((DEC_SEQ, SB, D_C), lambda s: (0, s, 0)),
            pl.BlockSpec((DEC_SEQ, SB, D_C), lambda s: (0, s, 0)),
        ],
        out_shape=[
            jax.ShapeDtypeStruct((DEC_SEQ, DEC_BATCH, D_C), BF16),
            jax.ShapeDtypeStruct((DEC_SEQ, DEC_BATCH, D_C), F32),
        ],
        compiler_params=_cparams(("parallel",)),
        name="gmlp_sample",
    )(ps, ng, nb, wsm, bsm)


def _softplus(x):
    return jnp.maximum(x, 0.0) + jnp.log1p(jnp.exp(-jnp.abs(x)))


def _l2norm(x):
    return x * lax.rsqrt(jnp.sum(x * x, axis=-1, keepdims=True) + EPS)


def _bdot(a, b):
    return _dot(a.astype(BF16), b.astype(BF16))


def _unit_lower_inverses(lmats, row, col):
    n = lmats[0].shape[0]
    eye = jnp.where(row == col, 1.0, 0.0).astype(F32)
    blk8 = (row >> 3) == (col >> 3)
    negs = [jnp.where(blk8, -l, 0.0) for l in lmats]
    ts = [eye + m for m in negs]
    p2 = [_bdot(m, m) for m in negs]
    ts = [t + _bdot(t, p) for t, p in zip(ts, p2)]
    p4 = [_bdot(p, p) for p in p2]
    ts = [t + _bdot(t, p) for t, p in zip(ts, p4)]
    sh = 3
    while (1 << sh) < n:
        sel = ((row >> (sh + 1)) == (col >> (sh + 1))) & ((row >> sh) != (col >> sh))
        offs = [jnp.where(sel, l, 0.0) for l in lmats]
        mid = [_bdot(t, o) for t, o in zip(ts, offs)]
        ts = [t - _bdot(m, t) for t, m in zip(ts, mid)]
        sh += 1
    return ts


def _delta_prompt_body(q_ref, k_ref, v_ref, z_ref, db_ref, cw_ref, alog_ref, dtb_ref, og_ref,
                       yb_ref, cb_ref, so_ref,
                       ext_ref, s_ref, qs_ref, ks_ref, vs_ref, gc_ref, gr_ref, bc_ref, os_ref):
    c = pl.program_id(1)
    n = CH_B
    hist = SUBLANES

    @pl.when(c == 0)
    def _():
        ext_ref[0:hist, :] = jnp.zeros((hist, 3 * D_B), F32)
        s_ref[...] = jnp.zeros((H_B, DK, DV), F32)

    ext_ref[hist:hist + n, 0:D_B] = q_ref[...]
    ext_ref[hist:hist + n, D_B:2 * D_B] = k_ref[...]
    ext_ref[hist:hist + n, 2 * D_B:] = v_ref[...]

    @pl.when(c == pl.num_programs(1) - 1)
    def _():
        cb_ref[0] = ext_ref[hist + n - (CONV_B_W - 1):hist + n, :]

    off = hist - (CONV_B_W - 1)
    acc = ext_ref[off:off + n, :] * cw_ref[0:1, :]
    for w in range(1, CONV_B_W):
        acc = acc + ext_ref[off + w:off + w + n, :] * cw_ref[w:w + 1, :]
    qkv = _silu(acc)
    ext_ref[0:hist, :] = ext_ref[n:n + hist, :]

    row = lax.broadcasted_iota(jnp.int32, (n, n), 0)
    col = lax.broadcasted_iota(jnp.int32, (n, n), 1)
    incl = row >= col

    db = db_ref[:, 0:LANES]
    g = -jnp.exp(alog_ref[...]) * _softplus(db + dtb_ref[...])
    beta = _sigmoid(db)
    gcs = _dot(jnp.where(incl, 1.0, 0.0).astype(F32), g, HIGHEST)
    gcs_t = gcs.T
    for h in range(H_B):
        hs = slice(h * DK, (h + 1) * DK)
        qs_ref[h] = _l2norm(qkv[:, hs]) * (DK ** -0.5)
        ks_ref[h] = _l2norm(qkv[:, D_B + h * DK:D_B + (h + 1) * DK])
        vs_ref[h] = qkv[:, 2 * D_B + h * DV:2 * D_B + (h + 1) * DV]
        gc_ref[h] = jnp.broadcast_to(gcs[:, h:h + 1], (n, n))
        gr_ref[h] = jnp.broadcast_to(gcs_t[h:h + 1, :], (n, n))
        bc_ref[h] = jnp.broadcast_to(beta[:, H_B + h:H_B + h + 1], (n, n))

    strict = row > col

    def head_group(grp, carry):
        hh = [grp * HEAD_GROUP + j for j in range(HEAD_GROUP)]
        q = [qs_ref[h] for h in hh]
        k = [ks_ref[h] for h in hh]
        gcol = [gc_ref[h] for h in hh]
        bcol = [bc_ref[h] for h in hh]
        decay = [jnp.where(incl, jnp.exp(jnp.where(incl, gc - gr_ref[h], 0.0)), 0.0)
                 for gc, h in zip(gcol, hh)]
        gam = [jnp.exp(gc) for gc in gcol]
        glast = [gc[n - 1:n, :] for gc in gcol]
        qkk = [_dot_nt(jnp.concatenate([qi, ki], axis=0).astype(BF16), ki.astype(BF16))
               for qi, ki in zip(q, k)]
        lmat = [jnp.where(strict, b * x[n:] * d, 0.0) for b, x, d in zip(bcol, qkk, decay)]
        tinv = _unit_lower_inverses(lmat, row, col)
        sol = [_bdot(t, jnp.concatenate([b * vs_ref[h], (b * gm) * ki], axis=1))
               for t, b, h, gm, ki in zip(tinv, bcol, hh, gam, k)]
        for j, h in enumerate(hh):
            s = s_ref[h]
            r = _bdot(jnp.concatenate([sol[j][:, DV:], q[j] * gam[j]], axis=0), s)
            ub = (sol[j][:, :DV] - r[:n]).astype(BF16)
            o = r[n:] + _dot((qkk[j][:n] * decay[j]).astype(BF16), ub)
            kd = k[j] * jnp.exp(glast[j] - gcol[j])
            s_ref[h] = jnp.exp(glast[j]) * s + _dot_tn(kd.astype(BF16), ub)
            os_ref[h] = _rms(o, og_ref[...])
        return carry

    lax.fori_loop(0, H_B // HEAD_GROUP, head_group, 0)

    for h in range(H_B):
        hs = slice(h * DV, (h + 1) * DV)
        yb_ref[:, hs] = (os_ref[h] * _silu(z_ref[:, hs])).astype(BF16)

    @pl.when(c == pl.num_programs(1) - 1)
    def _():
        so_ref[0] = s_ref[...]


def _delta_prompt(p, cw, alog, dtb, og):
    nc = SEQ // CH_B
    tile = pltpu.VMEM((H_B, CH_B, CH_B), F32)
    return pl.pallas_call(
        _delta_prompt_body,
        grid=(BATCH, nc),
        in_specs=[
            pl.BlockSpec((CH_B, COL_BLK), lambda bi, c: (bi * nc + c, 1)),
            pl.BlockSpec((CH_B, COL_BLK), lambda bi, c: (bi * nc + c, 2)),
            pl.BlockSpec((CH_B, COL_BLK), lambda bi, c: (bi * nc + c, 3)),
            pl.BlockSpec((CH_B, COL_BLK), lambda bi, c: (bi * nc + c, 4)),
            pl.BlockSpec((CH_B, SMALL_BLK), lambda bi, c: (bi * nc + c, DB_COL_BLK)),
            pl.BlockSpec((CONV_B_W, 3 * D_B), lambda bi, c: (0, 0)),
            pl.BlockSpec((1, LANES), lambda bi, c: (0, 0)),
            pl.BlockSpec((1, LANES), lambda bi, c: (0, 0)),
            pl.BlockSpec((1, DV), lambda bi, c: (0, 0)),
        ],
        out_specs=[
            pl.BlockSpec((CH_B, D_B), lambda bi, c: (bi * nc + c, 0)),
            pl.BlockSpec((1, CONV_B_W - 1, 3 * D_B), lambda bi, c: (bi, 0, 0)),
            pl.BlockSpec((1, H_B, DK, DV), lambda bi, c: (bi, 0, 0, 0)),
        ],
        out_shape=[
            jax.ShapeDtypeStruct((T_PROMPT, D_B), BF16),
            jax.ShapeDtypeStruct((BATCH, CONV_B_W - 1, 3 * D_B), F32),
            jax.ShapeDtypeStruct((BATCH, H_B, DK, DV), F32),
        ],
        scratch_shapes=[
            pltpu.VMEM((SUBLANES + CH_B, 3 * D_B), F32),
            pltpu.VMEM((H_B, DK, DV), F32),
            tile, tile, tile, tile, tile, tile, tile,
        ],
        compiler_params=_cparams(("parallel", "arbitrary")),
        name="delta_prompt",
    )(p, p, p, p, p, cw, alog, dtb, og)


def _delta_sample_body(q_ref, k_ref, v_ref, z_ref, db_ref, st_ref, s0_ref, cw_ref, alog_ref,
                       dtb_ref, og_ref, all_layers_hbm, yb_ref, cb_ref, so_ref):
    del all_layers_hbm
    ns = CONV_B_W - 1
    nt = DEC_SEQ

    def ext_part(r, lo, ref):
        return st_ref[r, :, lo:lo + D_B] if r < ns else ref[r - ns]

    def conv(lo, ref):
        out = []
        for t in range(nt):
            acc = ext_part(t, lo, ref) * cw_ref[0:1, lo:lo + D_B]
            for w in range(1, CONV_B_W):
                acc = acc + ext_part(t + w, lo, ref) * cw_ref[w:w + 1, lo:lo + D_B]
            out.append(_silu(acc))
        return out

    qc, kc, vc = conv(0, q_ref), conv(D_B, k_ref), conv(2 * D_B, v_ref)
    for r in range(ns):
        for lo, ref in ((0, q_ref), (D_B, k_ref), (2 * D_B, v_ref)):
            cb_ref[r, :, lo:lo + D_B] = ext_part(r + nt, lo, ref)

    g, beta = [], []
    for t in range(nt):
        db = db_ref[t][:, 0:LANES]
        g.append(-jnp.exp(alog_ref[...]) * _softplus(db + dtb_ref[...]))
        beta.append(_sigmoid(db))
    gc = [g[0]]
    for t in range(1, nt):
        gc.append(gc[t - 1] + g[t])

    rowid = lax.broadcasted_iota(jnp.int32, (SB, DV), 0)
    rows4 = lax.broadcasted_iota(jnp.int32, (nt * SB, DV), 0) & (SB - 1)

    for h in range(H_B):
        hs = slice(h * DK, (h + 1) * DK)
        q = [_l2norm(qc[t][:, hs]) * (DK ** -0.5) for t in range(nt)]
        k = [_l2norm(kc[t][:, hs]) for t in range(nt)]
        v = [vc[t][:, hs] for t in range(nt)]
        gch = [gc[t][:, h:h + 1] for t in range(nt)]
        bh = [beta[t][:, H_B + h:H_B + h + 1] for t in range(nt)]
        gam = [jnp.exp(x) for x in gch]

        lhs = jnp.concatenate(k + [q[t] * gam[t] for t in range(nt)], axis=0).astype(BF16)
        ks0 = [jnp.zeros((SB, DV), F32) for _ in range(nt)]
        qs0 = [jnp.zeros((SB, DV), F32) for _ in range(nt)]
        for s in range(SB):
            res = _dot(lhs, s0_ref[s, h].astype(BF16))
            for t in range(nt):
                ks0[t] = jnp.where(rowid == s, res[t * SB:(t + 1) * SB], ks0[t])
                qs0[t] = jnp.where(rowid == s, res[(nt + t) * SB:(nt + t + 1) * SB], qs0[t])

        u = []
        for t in range(nt):
            acc = bh[t] * (v[t] - gam[t] * ks0[t])
            for j in range(t):
                kk = jnp.sum(k[t] * k[j], axis=-1, keepdims=True)
                acc = acc - (bh[t] * kk * jnp.exp(gch[t] - gch[j])) * u[j]
            u.append(acc)

        for t in range(nt):
            o = qs0[t]
            for j in range(t + 1):
                qk = jnp.sum(q[t] * k[j], axis=-1, keepdims=True)
                o = o + (qk * jnp.exp(gch[t] - gch[j])) * u[j]
            yb_ref[t, :, hs] = (_rms(o, og_ref[...]) * _silu(z_ref[t][:, hs])).astype(BF16)

        kd = jnp.concatenate([k[t] * jnp.exp(gch[nt - 1] - gch[t]) for t in range(nt)],
                             axis=0).astype(BF16)
        uall = jnp.concatenate(u, axis=0)
        for s in range(SB):
            um = jnp.where(rows4 == s, uall, 0.0).astype(BF16)
            gl = jnp.exp(gch[nt - 1][s:s + 1, :])
            so_ref[s, h] = gl * s0_ref[s, h] + _dot_tn(kd, um)


def _delta_sample(ps, conv_state_t, delta_state, new_state, layer, cw, alog, dtb, og):
    ns = CONV_B_W - 1
    n_before = 11
    return pl.pallas_call(
        _delta_sample_body,
        grid=(DEC_BATCH // SB,),
        input_output_aliases={n_before: 2},
        in_specs=[
            pl.BlockSpec((DEC_SEQ, SB, COL_BLK), lambda s: (0, s, 1)),
            pl.BlockSpec((DEC_SEQ, SB, COL_BLK), lambda s: (0, s, 2)),
            pl.BlockSpec((DEC_SEQ, SB, COL_BLK), lambda s: (0, s, 3)),
            pl.BlockSpec((DEC_SEQ, SB, COL_BLK), lambda s: (0, s, 4)),
            pl.BlockSpec((DEC_SEQ, SB, SMALL_BLK), lambda s: (0, s, DB_COL_BLK)),
            pl.BlockSpec((ns, SB, 3 * D_B), lambda s: (0, s, 0)),
            pl.BlockSpec((SB, None, H_B, DK, DV), lambda s: (s, layer, 0, 0, 0)),
            pl.BlockSpec((CONV_B_W, 3 * D_B), lambda s: (0, 0)),
            pl.BlockSpec((1, LANES), lambda s: (0, 0)),
            pl.BlockSpec((1, LANES), lambda s: (0, 0)),
            pl.BlockSpec((1, DV), lambda s: (0, 0)),
            pl.BlockSpec(memory_space=pl.ANY),
        ],
        out_specs=[
            pl.BlockSpec((DEC_SEQ, SB, D_B), lambda s: (0, s, 0)),
            pl.BlockSpec((ns, SB, 3 * D_B), lambda s: (0, s, 0)),
            pl.BlockSpec((SB, None, H_B, DK, DV), lambda s: (s, layer, 0, 0, 0)),
        ],
        out_shape=[
            jax.ShapeDtypeStruct((DEC_SEQ, DEC_BATCH, D_B), BF16),
            jax.ShapeDtypeStruct((ns, DEC_BATCH, 3 * D_B), F32),
            jax.ShapeDtypeStruct((DEC_BATCH, DEPTH, H_B, DK, DV), F32),
        ],
        compiler_params=_cparams(("parallel",)),
        name="delta_sample",
    )(ps, ps, ps, ps, ps, conv_state_t, delta_state, cw, alog, dtb, og, new_state)


def _row(x):
    return x.reshape(1, -1)


def _pad_lanes(x):
    return jnp.pad(x, (0, LANES - x.shape[0])).reshape(1, LANES)


def kernel(x_prompt, x_sample, state_conv_a, state_conv_b, state_delta, norm_ffn1, ffn1_gate,
           ffn1_up, ffn1_down, norm_mix, w_in, conv_a_w, conv_a_b, norm_a_g, norm_a_b, conv_b_w,
           a_log, dt_bias, norm_o_g, norm_c_g, norm_c_b, sgu_w, sgu_b, w_out, norm_ffn2,
           ffn2_gate, ffn2_up, ffn2_down, norm_final):
    wg1, wu1, wd1 = ffn1_gate.astype(BF16), ffn1_up.astype(BF16), ffn1_down.astype(BF16)
    wg2, wu2, wd2 = ffn2_gate.astype(BF16), ffn2_up.astype(BF16), ffn2_down.astype(BF16)
    w_in_b = w_in.astype(BF16)
    w_uv_b = w_in[:, :, N_IN - 2 * D_C:].astype(BF16)
    w_out_b = w_out.astype(BF16)

    x = [x_prompt.reshape(T_PROMPT, D_MODEL),
         jnp.transpose(x_sample, (1, 0, 2)).reshape(T_SAMPLE, D_MODEL)]
    gfin = _row(norm_final)
    s_s = jnp.zeros((DEC_BATCH, DEPTH, H_B, DK, DV), F32)

    wsm = jnp.repeat(sgu_w[:, :, :DEC_SEQ, :DEC_SEQ].reshape(DEPTH, H_C, DEC_SEQ * DEC_SEQ),
                     HEAD_C, axis=1)
    wsm = jnp.transpose(wsm, (0, 2, 1))
    bsm = jnp.transpose(jnp.repeat(sgu_b[:, :, :SUBLANES], HEAD_C, axis=1), (0, 2, 1))
    bst = jnp.transpose(sgu_b, (0, 2, 1))

    outs = {k: [] for k in ("ap", "bp", "sp", "vp", "as", "bs", "ss", "vs")}
    for i in range(DEPTH):
        x = _ffn(x, _row(norm_ffn1[i]), wg1, wu1, wd1, gfin,
                 layer=i, split_out=False, final_norm=False)[0]
        p = _proj_in(x, _row(norm_mix[i]), w_in_b, w_uv_b, i)
        ps = p[T_PROMPT:].reshape(DEC_SEQ, DEC_BATCH, N_PROJ)

        caw, cab = conv_a_w[i], _row(conv_a_b[i])
        nag, nab = _row(norm_a_g[i]), _row(norm_a_b[i])
        ya_p, a_p = _conv_a_prompt(p, caw, cab, nag, nab)
        ya_s, a_s = _conv_a_sample(ps, jnp.transpose(state_conv_a[:, i], (1, 0, 2)),
                                   caw, cab, nag, nab)
        a_s = jnp.transpose(a_s, (1, 0, 2))

        alog, dtb, og = _pad_lanes(a_log[i]), _pad_lanes(dt_bias[i]), _row(norm_o_g[i])
        yb_p, b_p, s_p = _delta_prompt(p, conv_b_w[i], alog, dtb, og)
        yb_s, b_s, s_s = _delta_sample(ps, jnp.transpose(state_conv_b[:, i], (1, 0, 2)),
                                       state_delta, s_s, i, conv_b_w[i], alog, dtb, og)
        b_s = jnp.transpose(b_s, (1, 0, 2))

        ncg, ncb = _row(norm_c_g[i]), _row(norm_c_b[i])
        yc_p, v_p = _gmlp_prompt(p, ncg, ncb, sgu_w[i], bst[i])
        yc_s, v_s = _gmlp_sample(ps, ncg, ncb, wsm[i], bsm[i])

        y_s = [y.reshape(T_SAMPLE, y.shape[-1]) for y in (ya_s, yb_s, yc_s)]
        x = _proj_out(x, [ya_p, yb_p, yc_p], y_s, w_out_b, i)
        last = i == DEPTH - 1
        x = _ffn([x], _row(norm_ffn2[i]), wg2, wu2, wd2, gfin,
                 layer=i, split_out=last, final_norm=last)

        outs["ap"].append(a_p); outs["bp"].append(b_p); outs["sp"].append(s_p)
        outs["vp"].append(v_p); outs["as"].append(a_s); outs["bs"].append(b_s)
        outs["vs"].append(jnp.transpose(v_s, (1, 0, 2)))

    y_prompt = x[0].reshape(BATCH, SEQ, D_MODEL)
    y_sample = jnp.transpose(x[1].reshape(DEC_SEQ, DEC_BATCH, D_MODEL), (1, 0, 2))
    st = lambda k: jnp.stack(outs[k], axis=1)
    return (y_prompt, y_sample, st("ap"), st("bp"), st("sp"), st("vp"),
            st("as"), st("bs"), s_s, st("vs"))
```

```python
import functools

import numpy as np
import jax
import jax.numpy as jnp
from jax import lax
from jax.experimental import pallas as pl
from jax.experimental.pallas import tpu as pltpu

F32 = jnp.float32
BF16 = jnp.bfloat16
HIGHEST = lax.Precision.HIGHEST

D_MODEL = 4096
BATCH = 4
SEQ = 2048
DEPTH = 2
DEC_BATCH = 128
DEC_SEQ = 4
D_A = 1024
D_B = 2048
D_C = 1024
DK = 128
DV = 128
H_B = 16
H_C = 8
HEAD_C = 128
CONV_A_W = 31
CONV_B_W = 4
CHUNK_C = 128
D_FF = 11008
EPS = 1e-6

T_PROMPT = BATCH * SEQ
T_SAMPLE = DEC_BATCH * DEC_SEQ
T_ALL = T_PROMPT + T_SAMPLE

N_IN = 2 * D_A + 4 * D_B + 2 * H_B + 2 * D_C
N_MAIN = 2 * D_A + 4 * D_B
COL_BLK = 2048
SMALL_BLK = 512
TN_IN = 512
N_PROJ = N_MAIN + 2 * D_C + TN_IN
UV_COL_BLK = N_MAIN // COL_BLK
DB_COL_BLK = (N_MAIN + 2 * D_C) // SMALL_BLK
LANES = 128
SUBLANES = 8

VMEM_LIMIT = 56 * 1024 * 1024

TM = 512
DOWN_COLS = 1024
NORM_ROWS_FFN = 128
TM_IN = 1088
NORM_ROWS = 272
TF = 512
TN_OUT = 1024
TL_A = 256
TL_C = 256
CH_B = 128
HEAD_GROUP = 16
SB = 8


def _cparams(sem):
    return pltpu.CompilerParams(dimension_semantics=sem, vmem_limit_bytes=VMEM_LIMIT)


def _sigmoid(x):
    return jax.nn.sigmoid(x)


def _silu(x):
    return x * _sigmoid(x)


def _rms(x, g):
    return x * lax.rsqrt(jnp.mean(x * x, axis=-1, keepdims=True) + EPS) * g


def _layernorm(x, g, b):
    mu = jnp.mean(x, axis=-1, keepdims=True)
    xc = x - mu
    var = jnp.mean(xc * xc, axis=-1, keepdims=True)
    return xc * lax.rsqrt(var + EPS) * g + b


def _dot(a, b, precision=None):
    return jnp.dot(a, b, preferred_element_type=F32, precision=precision)


def _dot_nt(a, b, precision=None):
    return lax.dot_general(a, b, (((1,), (1,)), ((), ())), preferred_element_type=F32,
                           precision=precision)


def _dot_tn(a, b, precision=None):
    return lax.dot_general(a, b, (((0,), (0,)), ((), ())), preferred_element_type=F32,
                           precision=precision)


N_PROMPT_TILES = T_PROMPT // TM
N_TILES = T_ALL // TM


def _ffn_body(*refs, split_in, split_out, layer, final_norm):
    n_in = 2 if split_in else 1
    n_out = 2 if split_out else 1
    x_hbm = refs[:n_in]
    g_ref, wg_hbm, wu_hbm, wd_hbm, gf_ref = refs[n_in:n_in + 5]
    o_hbm = refs[n_in + 5:n_in + 5 + n_out]
    (acc_ref, h_ref, wg_buf, wu_buf, wd_buf, ld_sem, wb_sem,
     w_sem) = refs[n_in + 5 + n_out:]
    i = pl.program_id(0)
    ni = pl.num_programs(0)
    slot = i % 2

    def tile_dma(tile, sl, arrays, sem, to_vmem, op):
        def run(hbm, row0):
            rows = hbm.at[pl.ds(pl.multiple_of(row0, TM), TM)]
            src, dst = (rows, acc_ref.at[sl]) if to_vmem else (acc_ref.at[sl], rows)
            cp = pltpu.make_async_copy(src, dst, sem.at[sl])
            cp.start() if op == "start" else cp.wait()

        if len(arrays) == 2:
            @pl.when(tile < N_PROMPT_TILES)
            def _():
                run(arrays[0], tile * TM)

            @pl.when(tile >= N_PROMPT_TILES)
            def _():
                run(arrays[1], (tile - N_PROMPT_TILES) * TM)
        else:
            run(arrays[0], tile * TM)

    load = functools.partial(tile_dma, arrays=x_hbm, sem=ld_sem, to_vmem=True)
    writeback = functools.partial(tile_dma, arrays=o_hbm, sem=wb_sem, to_vmem=False)

    d_ff = wg_hbm.shape[2]
    n_wide = d_ff // TF
    tail = d_ff - n_wide * TF
    n_steps = n_wide + 1

    def weight_dma(jt, ws, op):
        def run(c0, width):
            copies = (
                (wg_hbm.at[layer, :, pl.ds(c0, width)], wg_buf.at[ws, :, pl.ds(0, width)], 0),
                (wu_hbm.at[layer, :, pl.ds(c0, width)], wu_buf.at[ws, :, pl.ds(0, width)], 1),
                (wd_hbm.at[layer, pl.ds(c0, width), :], wd_buf.at[ws, pl.ds(0, width), :], 2),
            )
            for src, dst, which in copies:
                cp = pltpu.make_async_copy(src, dst, w_sem.at[which, ws])
                cp.start() if op == "start" else cp.wait()

        if isinstance(jt, int):
            run(jt * TF, TF if jt < n_wide else tail)
        else:
            @pl.when(jt < n_wide)
            def _():
                run(pl.multiple_of(jt * TF, TF), TF)

            @pl.when(jt == n_wide)
            def _():
                run(n_wide * TF, tail)

    def hidden_tile(ws, width):
        h = h_ref[...]
        hg = _dot(h, wg_buf[ws, :, 0:width])
        hu = _dot(h, wu_buf[ws, :, 0:width])
        a = (0.5 * _silu(hg) * hu).astype(BF16)
        for c0 in range(0, D_MODEL, DOWN_COLS):
            acc_ref[slot, :, c0:c0 + DOWN_COLS] += _dot(a, wd_buf[ws, 0:width, c0:c0 + DOWN_COLS])

    @pl.when(i == 0)
    def _():
        load(i, slot, op="start")
        weight_dma(0, 0, "start")

    load(i, slot, op="wait")
    def by_row_chunks(fn):
        def chunk(c, carry):
            fn(pl.ds(pl.multiple_of(c * NORM_ROWS_FFN, NORM_ROWS_FFN), NORM_ROWS_FFN))
            return carry

        lax.fori_loop(0, TM // NORM_ROWS_FFN, chunk, 0)

    def set_h(rows):
        h_ref[rows, :] = _rms(acc_ref[slot, rows, :], g_ref[...]).astype(BF16)

    by_row_chunks(set_h)

    def step(j, carry):
        ws = (i * n_steps + j) % 2
        weight_dma(j, ws, "wait")
        weight_dma(j + 1, 1 - ws, "start")

        @pl.when((j == 1) & (i + 1 < ni))
        def _():
            @pl.when(i >= 1)
            def _():
                writeback(i - 1, 1 - slot, op="wait")

            load(i + 1, 1 - slot, op="start")

        hidden_tile(ws, TF)
        return carry

    lax.fori_loop(0, n_wide, step, 0)

    ws_tail = (i * n_steps + n_wide) % 2
    weight_dma(n_wide, ws_tail, "wait")

    @pl.when(i + 1 < ni)
    def _():
        weight_dma(0, 1 - ws_tail, "start")

    hidden_tile(ws_tail, tail)

    if final_norm:
        def set_final(rows):
            acc_ref[slot, rows, :] = _rms(acc_ref[slot, rows, :], gf_ref[...])

        by_row_chunks(set_final)
    writeback(i, slot, op="start")

    @pl.when(i == ni - 1)
    def _():
        writeback(i - 1, 1 - slot, op="wait")
        writeback(i, slot, op="wait")


def _ffn(xs, g, wg, wu, wd, gf, *, layer, split_out, final_norm):
    tail = wg.shape[2] % TF
    assert N_TILES >= 2 and wg.shape[2] // TF >= 2 and tail > 0 and tail % LANES == 0
    split_in = len(xs) == 2
    if split_out:
        out_shape = [jax.ShapeDtypeStruct((T_PROMPT, D_MODEL), F32),
                     jax.ShapeDtypeStruct((T_SAMPLE, D_MODEL), F32)]
    else:
        out_shape = [jax.ShapeDtypeStruct((T_ALL, D_MODEL), F32)]
    hbm = pl.BlockSpec(memory_space=pl.ANY)
    vec = pl.BlockSpec((1, D_MODEL), lambda i: (0, 0))
    return pl.pallas_call(
        functools.partial(_ffn_body, split_in=split_in, split_out=split_out, layer=layer,
                          final_norm=final_norm),
        grid=(N_TILES,),
        in_specs=[hbm] * len(xs) + [vec, hbm, hbm, hbm, vec],
        out_specs=[hbm] * len(out_shape),
        out_shape=out_shape,
        scratch_shapes=[
            pltpu.VMEM((2, TM, D_MODEL), F32),
            pltpu.VMEM((TM, D_MODEL), BF16),
            pltpu.VMEM((2, D_MODEL, TF), BF16),
            pltpu.VMEM((2, D_MODEL, TF), BF16),
            pltpu.VMEM((2, TF, D_MODEL), BF16),
            pltpu.SemaphoreType.DMA((2,)),
            pltpu.SemaphoreType.DMA((2,)),
            pltpu.SemaphoreType.DMA((3, 2)),
        ],
        compiler_params=_cparams(("arbitrary",)),
        name="ffn_final" if final_norm else "ffn",
    )(*xs, g, wg, wu, wd, gf)


N_MAIN_TILES = N_MAIN // TN_IN
N_UV_TILES = 2 * D_C // TN_IN


def _proj_in_body(x_ref, g_ref, w_ref, wuv_ref, o_ref, h_ref):
    j = pl.program_id(1)

    @pl.when(j == 0)
    def _():
        def chunk(c, carry):
            rows = pl.ds(pl.multiple_of(c * NORM_ROWS, NORM_ROWS), NORM_ROWS)
            h_ref[rows, :] = _rms(x_ref[rows, :], g_ref[...]).astype(BF16)
            return carry

        lax.fori_loop(0, TM_IN // NORM_ROWS, chunk, 0)

    is_uv =(j >= N_MAIN_TILES) & (j < N_MAIN_TILES + N_UV_TILES)

    @pl.when(is_uv)
    def _():
        o_ref[...] = _dot_nt(h_ref[...], wuv_ref[...])

    @pl.when(jnp.logical_not(is_uv))
    def _():
        o_ref[...] = _dot_nt(h_ref[...], w_ref[...])


def _proj_in(x, g, w, w_uv, layer):
    t = x.shape[0]

    def main_tile(i, j):
        return (layer, jnp.minimum(j, N_MAIN_TILES), 0)

    def uv_tile(i, j):
        return (layer, jnp.clip(j - N_MAIN_TILES, 0, N_UV_TILES - 1), 0)

    return pl.pallas_call(
        _proj_in_body,
        grid=(t // TM_IN, N_PROJ // TN_IN),
        in_specs=[
            pl.BlockSpec((TM_IN, D_MODEL), lambda i, j: (i, 0), pipeline_mode=pl.Buffered(1)),
            pl.BlockSpec((1, D_MODEL), lambda i, j: (0, 0)),
            pl.BlockSpec((None, TN_IN, D_MODEL), main_tile),
            pl.BlockSpec((None, TN_IN, D_MODEL), uv_tile),
        ],
        out_specs=pl.BlockSpec((TM_IN, TN_IN), lambda i, j: (i, j)),
        out_shape=jax.ShapeDtypeStruct((t, N_PROJ), F32),
        scratch_shapes=[pltpu.VMEM((TM_IN, D_MODEL), BF16)],
        compiler_params=_cparams(("parallel", "arbitrary")),
        name="proj_in",
    )(x, g, w, w_uv)


def _proj_out_body(x_ref, yap_ref, ybp_ref, ycp_ref, yas_ref, ybs_ref, ycs_ref, w_ref, o_ref):
    def mix(ya_ref, yb_ref, yc_ref):
        acc = x_ref[...]
        acc += _dot(ya_ref[...], w_ref[0:D_A, :])
        acc += _dot(yb_ref[...], w_ref[D_A:D_A + D_B, :])
        acc += _dot(yc_ref[...], w_ref[D_A + D_B:, :])
        o_ref[...] = acc

    is_prompt = pl.program_id(0) < N_PROMPT_TILES

    @pl.when(is_prompt)
    def _():
        mix(yap_ref, ybp_ref, ycp_ref)

    @pl.when(jnp.logical_not(is_prompt))
    def _():
        mix(yas_ref, ybs_ref, ycs_ref)


def _proj_out(x, y_prompt, y_sample, w, layer):
    assert T_SAMPLE == TM

    def prompt_spec(width):
        return pl.BlockSpec((TM, width), lambda i, j: (jnp.minimum(i, N_PROMPT_TILES - 1), 0))

    def sample_spec(width):
        return pl.BlockSpec((TM, width), lambda i, j: (0, 0))

    widths = (D_A, D_B, D_C)
    return pl.pallas_call(
        _proj_out_body,
        grid=(N_TILES, D_MODEL // TN_OUT),
        in_specs=[pl.BlockSpec((TM, TN_OUT), lambda i, j: (i, j))]
        + [prompt_spec(wd) for wd in widths] + [sample_spec(wd) for wd in widths]
        + [pl.BlockSpec((None, D_MODEL, TN_OUT), lambda i, j: (layer, 0, j))],
        out_specs=pl.BlockSpec((TM, TN_OUT), lambda i, j: (i, j)),
        out_shape=jax.ShapeDtypeStruct((T_ALL, D_MODEL), F32),
        compiler_params=_cparams(("parallel", "arbitrary")),
        name="proj_out",
    )(x, *y_prompt, *y_sample, w)


HIST = 32


RB_A = 32


def _conv_a_prompt_body(pa_ref, w_ref, b_ref, lg_ref, lb_ref, ya_ref, st_ref,
                        ext_ref, sh_ref, conv_ref):
    l = pl.program_id(1)

    @pl.when(l == 0)
    def _():
        ext_ref[0:HIST, :] = jnp.zeros((HIST, D_A), F32)

    pa = pa_ref[...]
    ext_ref[HIST:HIST + TL_A, :] = pa[:, :D_A] * _sigmoid(pa[:, D_A:])
    n_sh = HIST + TL_A - SUBLANES
    for r in range(1, SUBLANES):
        sh_ref[r - 1] = ext_ref[r:r + n_sh, :]
    off = HIST - (CONV_A_W - 1)

    def block(rb, carry):
        base = rb * RB_A
        acc = jnp.broadcast_to(b_ref[...], (RB_A, D_A))
        for w in range(CONV_A_W):
            r = (off + w) % SUBLANES
            rows = pl.ds(pl.multiple_of(base + (off + w - r), SUBLANES), RB_A)
            x = ext_ref[rows, :] if r == 0 else sh_ref[r - 1, rows, :]
            acc = acc + x * w_ref[w:w + 1, :]
        conv_ref[pl.ds(pl.multiple_of(base, RB_A), RB_A), :] = acc
        return carry

    lax.fori_loop(0, TL_A // RB_A, block, 0)
    ya_ref[...] = _silu(_layernorm(conv_ref[...], lg_ref[...], lb_ref[...])).astype(BF16)

    @pl.when(l == pl.num_programs(1) - 1)
    def _():
        st_ref[0] = ext_ref[HIST + TL_A - (CONV_A_W - 1):HIST + TL_A, :]

    ext_ref[0:HIST, :] = ext_ref[TL_A:TL_A + HIST, :]


def _conv_a_prompt(p, w, b, lg, lb):
    nl = SEQ // TL_A
    return pl.pallas_call(
        _conv_a_prompt_body,
        grid=(BATCH, nl),
        in_specs=[
            pl.BlockSpec((TL_A, COL_BLK), lambda bi, l: (bi * nl + l, 0)),
            pl.BlockSpec((CONV_A_W, D_A), lambda bi, l: (0, 0)),
            pl.BlockSpec((1, D_A), lambda bi, l: (0, 0)),
            pl.BlockSpec((1, D_A), lambda bi, l: (0, 0)),
            pl.BlockSpec((1, D_A), lambda bi, l: (0, 0)),
        ],
        out_specs=[
            pl.BlockSpec((TL_A, D_A), lambda bi, l: (bi * nl + l, 0)),
            pl.BlockSpec((1, CONV_A_W - 1, D_A), lambda bi, l: (bi, 0, 0)),
        ],
        out_shape=[
            jax.ShapeDtypeStruct((T_PROMPT, D_A), BF16),
            jax.ShapeDtypeStruct((BATCH, CONV_A_W - 1, D_A), F32),
        ],
        scratch_shapes=[
            pltpu.VMEM((HIST + TL_A, D_A), F32),
            pltpu.VMEM((SUBLANES - 1, HIST + TL_A - SUBLANES, D_A), F32),
            pltpu.VMEM((TL_A, D_A), F32),
        ],
        compiler_params=_cparams(("parallel", "arbitrary")),
        name="conv_a_prompt",
    )(p, w, b, lg, lb)


def _conv_a_sample_body(pa_ref, st_ref, w_ref, b_ref, lg_ref, lb_ref, ya_ref, ns_ref):
    ns = CONV_A_W - 1
    ga = []
    for t in range(DEC_SEQ):
        pa = pa_ref[t]
        ga.append(pa[:, :D_A] * _sigmoid(pa[:, D_A:]))

    def ext_row(r):
        return st_ref[r] if r < ns else ga[r - ns]

    for t in range(DEC_SEQ):
        acc = jnp.broadcast_to(b_ref[...], (SB, D_A))
        for w in range(CONV_A_W):
            acc = acc + ext_row(t + w) * w_ref[w:w + 1, :]
        ya_ref[t] = _silu(_layernorm(acc, lg_ref[...], lb_ref[...])).astype(BF16)
    for r in range(ns):
        ns_ref[r] = ext_row(r + DEC_SEQ)


def _conv_a_sample(ps, state_t, w, b, lg, lb):
    ns = CONV_A_W - 1
    return pl.pallas_call(
        _conv_a_sample_body,
        grid=(DEC_BATCH // SB,),
        in_specs=[
            pl.BlockSpec((DEC_SEQ, SB, COL_BLK), lambda s: (0, s, 0)),
            pl.BlockSpec((ns, SB, D_A), lambda s: (0, s, 0)),
            pl.BlockSpec((CONV_A_W, D_A), lambda s: (0, 0)),
            pl.BlockSpec((1, D_A), lambda s: (0, 0)),
            pl.BlockSpec((1, D_A), lambda s: (0, 0)),
            pl.BlockSpec((1, D_A), lambda s: (0, 0)),
        ],
        out_specs=[
            pl.BlockSpec((DEC_SEQ, SB, D_A), lambda s: (0, s, 0)),
            pl.BlockSpec((ns, SB, D_A), lambda s: (0, s, 0)),
        ],
        out_shape=[
            jax.ShapeDtypeStruct((DEC_SEQ, DEC_BATCH, D_A), BF16),
            jax.ShapeDtypeStruct((ns, DEC_BATCH, D_A), F32),
        ],
        compiler_params=_cparams(("parallel",)),
        name="conv_a_sample",
    )(ps, state_t, w, b, lg, lb)


_SQRT_HALF = float(np.sqrt(0.5))


def _gelu(x):
    return 0.5 * x * (1.0 + lax.erf(x * _SQRT_HALF))


def _gmlp_prompt_body(puv_ref, ng_ref, nb_ref, ws_ref, bst_ref, yc_ref, cv_ref):
    l = pl.program_id(1)
    uv = _gelu(puv_ref[...])
    u = uv[:, :D_C]
    vv = _layernorm(uv[:, D_C:], ng_ref[...], nb_ref[...])
    row = lax.broadcasted_iota(jnp.int32, (CHUNK_C, CHUNK_C), 0)
    col = lax.broadcasted_iota(jnp.int32, (CHUNK_C, CHUNK_C), 1)
    vb = vv.astype(BF16)
    for h in range(H_C):
        ws = jnp.where(row >= col, ws_ref[h], 0.0).astype(BF16)
        hs = slice(h * HEAD_C, (h + 1) * HEAD_C)
        for c in range(TL_C // CHUNK_C):
            rs = slice(c * CHUNK_C, (c + 1) * CHUNK_C)
            mixed = _dot(ws, vb[rs, hs]) + bst_ref[:, h:h + 1]
            yc_ref[rs, hs] = (u[rs, hs] * mixed).astype(BF16)

    @pl.when(l == pl.num_programs(1) - 1)
    def _():
        cv_ref[0] = vv[TL_C - CHUNK_C:, :]


def _gmlp_prompt(p, ng, nb, ws, bst):
    nl = SEQ // TL_C
    return pl.pallas_call(
        _gmlp_prompt_body,
        grid=(BATCH, nl),
        in_specs=[
            pl.BlockSpec((TL_C, COL_BLK), lambda bi, l: (bi * nl + l, UV_COL_BLK)),
            pl.BlockSpec((1, D_C), lambda bi, l: (0, 0)),
            pl.BlockSpec((1, D_C), lambda bi, l: (0, 0)),
            pl.BlockSpec((H_C, CHUNK_C, CHUNK_C), lambda bi, l: (0, 0, 0)),
            pl.BlockSpec((CHUNK_C, H_C), lambda bi, l: (0, 0)),
        ],
        out_specs=[
            pl.BlockSpec((TL_C, D_C), lambda bi, l: (bi * nl + l, 0)),
            pl.BlockSpec((1, CHUNK_C, D_C), lambda bi, l: (bi, 0, 0)),
        ],
        out_shape=[
            jax.ShapeDtypeStruct((T_PROMPT, D_C), BF16),
            jax.ShapeDtypeStruct((BATCH, CHUNK_C, D_C), F32),
        ],
        compiler_params=_cparams(("parallel", "arbitrary")),
        name="gmlp_prompt",
    )(p, ng, nb, ws, bst)


def _gmlp_sample_body(puv_ref, ng_ref, nb_ref, wsm_ref, bsm_ref, yc_ref, cv_ref):
    u, vv = [], []
    for t in range(DEC_SEQ):
        uv = _gelu(puv_ref[t])
        u.append(uv[:, :D_C])
        vt = _layernorm(uv[:, D_C:], ng_ref[...], nb_ref[...])
        vv.append(vt)
        cv_ref[t] = vt
    for i in range(DEC_SEQ):
        mixed = jnp.broadcast_to(bsm_ref[i:i + 1, :], (SB, D_C))
        for j in range(i + 1):
            mixed = mixed + wsm_ref[i * DEC_SEQ + j:i * DEC_SEQ + j + 1, :] * vv[j]
        yc_ref[i] = (u[i] * mixed).astype(BF16)


def _gmlp_sample(ps, ng, nb, wsm, bsm):
    return pl.pallas_call(
        _gmlp_sample_body,
        grid=(DEC_BATCH // SB,),
        in_specs=[
            pl.BlockSpec((DEC_SEQ, SB, COL_BLK), lambda s: (0, s, UV_COL_BLK)),
            pl.BlockSpec((1, D_C), lambda s: (0, 0)),
            pl.BlockSpec((1, D_C), lambda s: (0, 0)),
            pl.BlockSpec((DEC_SEQ * DEC_SEQ, D_C), lambda s: (0, 0)),
            pl.BlockSpec((SUBLANES, D_C), lambda s: (0, 0)),
        ],
        out_specs=[
            pl.BlockSpec((DEC_SEQ, SB, D_C), lambda s: (0, s, 0)),
            pl.BlockSpec((DEC_SEQ, SB, D_C), lambda s: (0, s, 0)),
        ],
        out_shape=[
            jax.ShapeDtypeStruct((DEC_SEQ, DEC_BATCH, D_C), BF16),
            jax.ShapeDtypeStruct((DEC_SEQ, DEC_BATCH, D_C), F32),
        ],
        compiler_params=_cparams(("parallel",)),
        name="gmlp_sample",
    )(ps, ng, nb, wsm, bsm)


def _softplus(x):
    return jnp.maximum(x, 0.0) + jnp.log1p(jnp.exp(-jnp.abs(x)))


def _l2norm(x):
    return x * lax.rsqrt(jnp.sum(x * x, axis=-1, keepdims=True) + EPS)


def _bdot(a, b):
    return _dot(a.astype(BF16), b.astype(BF16))


def _unit_lower_inverses(lmats, row, col):
    n = lmats[0].shape[0]
    eye = jnp.where(row == col, 1.0, 0.0).astype(F32)
    blk8 = (row >> 3) == (col >> 3)
    negs = [jnp.where(blk8, -l, 0.0) for l in lmats]
    ts = [eye + m for m in negs]
    p2 = [_bdot(m, m) for m in negs]
    ts = [t + _bdot(t, p) for t, p in zip(ts, p2)]
    p4 = [_bdot(p, p) for p in p2]
    ts = [t + _bdot(t, p) for t, p in zip(ts, p4)]
    sh = 3
    while (1 << sh) < n:
        sel = ((row >> (sh + 1)) == (col >> (sh + 1))) & ((row >> sh) != (col >> sh))
        offs = [jnp.where(sel, l, 0.0) for l in lmats]
        mid = [_bdot(t, o) for t, o in zip(ts, offs)]
        ts = [t - _bdot(m, t) for t, m in zip(ts, mid)]
        sh += 1
    return ts


def _delta_prompt_body(q_ref, k_ref, v_ref, z_ref, db_ref, cw_ref, alog_ref, dtb_ref, og_ref,
                       yb_ref, cb_ref, so_ref,
                       ext_ref, s_ref, qs_ref, ks_ref, vs_ref, gc_ref, gr_ref, bc_ref, os_ref):
    c = pl.program_id(1)
    n = CH_B
    hist = SUBLANES

    @pl.when(c == 0)
    def _():
        ext_ref[0:hist, :] = jnp.zeros((hist, 3 * D_B), F32)
        s_ref[...] = jnp.zeros((H_B, DK, DV), F32)

    ext_ref[hist:hist + n, 0:D_B] = q_ref[...]
    ext_ref[hist:hist + n, D_B:2 * D_B] = k_ref[...]
    ext_ref[hist:hist + n, 2 * D_B:] = v_ref[...]

    @pl.when(c == pl.num_programs(1) - 1)
    def _():
        cb_ref[0] = ext_ref[hist + n - (CONV_B_W - 1):hist + n, :]

    off = hist - (CONV_B_W - 1)
    acc = ext_ref[off:off + n, :] * cw_ref[0:1, :]
    for w in range(1, CONV_B_W):
        acc = acc + ext_ref[off + w:off + w + n, :] * cw_ref[w:w + 1, :]
    qkv = _silu(acc)
    ext_ref[0:hist, :] = ext_ref[n:n + hist, :]

    row = lax.broadcasted_iota(jnp.int32, (n, n), 0)
    col = lax.broadcasted_iota(jnp.int32, (n, n), 1)
    incl = row >= col

    db = db_ref[:, 0:LANES]
    g = -jnp.exp(alog_ref[...]) * _softplus(db + dtb_ref[...])
    beta = _sigmoid(db)
    gcs = _dot(jnp.where(incl, 1.0, 0.0).astype(F32), g, HIGHEST)
    gcs_t = gcs.T
    for h in range(H_B):
        hs = slice(h * DK, (h + 1) * DK)
        qs_ref[h] = _l2norm(qkv[:, hs]) * (DK ** -0.5)
        ks_ref[h] = _l2norm(qkv[:, D_B + h * DK:D_B + (h + 1) * DK])
        vs_ref[h] = qkv[:, 2 * D_B + h * DV:2 * D_B + (h + 1) * DV]
        gc_ref[h] = jnp.broadcast_to(gcs[:, h:h + 1], (n, n))
        gr_ref[h] = jnp.broadcast_to(gcs_t[h:h + 1, :], (n, n))
        bc_ref[h] = jnp.broadcast_to(beta[:, H_B + h:H_B + h + 1], (n, n))

    strict = row > col

    def head_group(grp, carry):
        hh = [grp * HEAD_GROUP + j for j in range(HEAD_GROUP)]
        q = [qs_ref[h] for h in hh]
        k = [ks_ref[h] for h in hh]
        gcol = [gc_ref[h] for h in hh]
        bcol = [bc_ref[h] for h in hh]
        decay = [jnp.where(incl, jnp.exp(jnp.where(incl, gc - gr_ref[h], 0.0)), 0.0)
                 for gc, h in zip(gcol, hh)]
        gam = [jnp.exp(gc) for gc in gcol]
        glast = [gc[n - 1:n, :] for gc in gcol]
        qkk = [_dot_nt(jnp.concatenate([qi, ki], axis=0).astype(BF16), ki.astype(BF16))
               for qi, ki in zip(q, k)]
        lmat = [jnp.where(strict, b * x[n:] * d, 0.0) for b, x, d in zip(bcol, qkk, decay)]
        tinv = _unit_lower_inverses(lmat, row, col)
        sol = [_bdot(t, jnp.concatenate([b * vs_ref[h], (b * gm) * ki], axis=1))
               for t, b, h, gm, ki in zip(tinv, bcol, hh, gam, k)]
        for j, h in enumerate(hh):
            s = s_ref[h]
            r = _bdot(jnp.concatenate([sol[j][:, DV:], q[j] * gam[j]], axis=0), s)
            ub = (sol[j][:, :DV] - r[:n]).astype(BF16)
            o = r[n:] + _dot((qkk[j][:n] * decay[j]).astype(BF16), ub)
            kd = k[j] * jnp.exp(glast[j] - gcol[j])
            s_ref[h] = jnp.exp(glast[j]) * s + _dot_tn(kd.astype(BF16), ub)
            os_ref[h] = _rms(o, og_ref[...])
        return carry

    lax.fori_loop(0, H_B // HEAD_GROUP, head_group, 0)

    for h in range(H_B):
        hs = slice(h * DV, (h + 1) * DV)
        yb_ref[:, hs] = (os_ref[h] * _silu(z_ref[:, hs])).astype(BF16)

    @pl.when(c == pl.num_programs(1) - 1)
    def _():
        so_ref[0] = s_ref[...]


def _delta_prompt(p, cw, alog, dtb, og):
    nc = SEQ // CH_B
    tile = pltpu.VMEM((H_B, CH_B, CH_B), F32)
    return pl.pallas_call(
        _delta_prompt_body,
        grid=(BATCH, nc),
        in_specs=[
            pl.BlockSpec((CH_B, COL_BLK), lambda bi, c: (bi * nc + c, 1)),
            pl.BlockSpec((CH_B, COL_BLK), lambda bi, c: (bi * nc + c, 2)),
            pl.BlockSpec((CH_B, COL_BLK), lambda bi, c: (bi * nc + c, 3)),
            pl.BlockSpec((CH_B, COL_BLK), lambda bi, c: (bi * nc + c, 4)),
            pl.BlockSpec((CH_B, SMALL_BLK), lambda bi, c: (bi * nc + c, DB_COL_BLK)),
            pl.BlockSpec((CONV_B_W, 3 * D_B), lambda bi, c: (0, 0)),
            pl.BlockSpec((1, LANES), lambda bi, c: (0, 0)),
            pl.BlockSpec((1, LANES), lambda bi, c: (0, 0)),
            pl.BlockSpec((1, DV), lambda bi, c: (0, 0)),
        ],
        out_specs=[
            pl.BlockSpec((CH_B, D_B), lambda bi, c: (bi * nc + c, 0)),
            pl.BlockSpec((1, CONV_B_W - 1, 3 * D_B), lambda bi, c: (bi, 0, 0)),
            pl.BlockSpec((1, H_B, DK, DV), lambda bi, c: (bi, 0, 0, 0)),
        ],
        out_shape=[
            jax.ShapeDtypeStruct((T_PROMPT, D_B), BF16),
            jax.ShapeDtypeStruct((BATCH, CONV_B_W - 1, 3 * D_B), F32),
            jax.ShapeDtypeStruct((BATCH, H_B, DK, DV), F32),
        ],
        scratch_shapes=[
            pltpu.VMEM((SUBLANES + CH_B, 3 * D_B), F32),
            pltpu.VMEM((H_B, DK, DV), F32),
            tile, tile, tile, tile, tile, tile, tile,
        ],
        compiler_params=_cparams(("parallel", "arbitrary")),
        name="delta_prompt",
    )(p, p, p, p, p, cw, alog, dtb, og)


def _delta_sample_body(q_ref, k_ref, v_ref, z_ref, db_ref, st_ref, s0_ref, cw_ref, alog_ref,
                       dtb_ref, og_ref, all_layers_hbm, yb_ref, cb_ref, so_ref):
    del all_layers_hbm
    ns = CONV_B_W - 1
    nt = DEC_SEQ

    def ext_part(r, lo, ref):
        return st_ref[r, :, lo:lo + D_B] if r < ns else ref[r - ns]

    def conv(lo, ref):
        out = []
        for t in range(nt):
            acc = ext_part(t, lo, ref) * cw_ref[0:1, lo:lo + D_B]
            for w in range(1, CONV_B_W):
                acc = acc + ext_part(t + w, lo, ref) * cw_ref[w:w + 1, lo:lo + D_B]
            out.append(_silu(acc))
        return out

    qc, kc, vc = conv(0, q_ref), conv(D_B, k_ref), conv(2 * D_B, v_ref)
    for r in range(ns):
        for lo, ref in ((0, q_ref), (D_B, k_ref), (2 * D_B, v_ref)):
            cb_ref[r, :, lo:lo + D_B] = ext_part(r + nt, lo, ref)

    g, beta = [], []
    for t in range(nt):
        db = db_ref[t][:, 0:LANES]
        g.append(-jnp.exp(alog_ref[...]) * _softplus(db + dtb_ref[...]))
        beta.append(_sigmoid(db))
    gc = [g[0]]
    for t in range(1, nt):
        gc.append(gc[t - 1] + g[t])

    rowid = lax.broadcasted_iota(jnp.int32, (SB, DV), 0)
    rows4 = lax.broadcasted_iota(jnp.int32, (nt * SB, DV), 0) & (SB - 1)

    for h in range(H_B):
        hs = slice(h * DK, (h + 1) * DK)
        q = [_l2norm(qc[t][:, hs]) * (DK ** -0.5) for t in range(nt)]
        k = [_l2norm(kc[t][:, hs]) for t in range(nt)]
        v = [vc[t][:, hs] for t in range(nt)]
        gch = [gc[t][:, h:h + 1] for t in range(nt)]
        bh = [beta[t][:, H_B + h:H_B + h + 1] for t in range(nt)]
        gam = [jnp.exp(x) for x in gch]

        lhs = jnp.concatenate(k + [q[t] * gam[t] for t in range(nt)], axis=0).astype(BF16)
        ks0 = [jnp.zeros((SB, DV), F32) for _ in range(nt)]
        qs0 = [jnp.zeros((SB, DV), F32) for _ in range(nt)]
        for s in range(SB):
            res = _dot(lhs, s0_ref[s, h].astype(BF16))
            for t in range(nt):
                ks0[t] = jnp.where(rowid == s, res[t * SB:(t + 1) * SB], ks0[t])
                qs0[t] = jnp.where(rowid == s, res[(nt + t) * SB:(nt + t + 1) * SB], qs0[t])

        u = []
        for t in range(nt):
            acc = bh[t] * (v[t] - gam[t] * ks0[t])
            for j in range(t):
                kk = jnp.sum(k[t] * k[j], axis=-1, keepdims=True)
                acc = acc - (bh[t] * kk * jnp.exp(gch[t] - gch[j])) * u[j]
            u.append(acc)

        for t in range(nt):
            o = qs0[t]
            for j in range(t + 1):
                qk = jnp.sum(q[t] * k[j], axis=-1, keepdims=True)
                o = o + (qk * jnp.exp(gch[t] - gch[j])) * u[j]
            yb_ref[t, :, hs] = (_rms(o, og_ref[...]) * _silu(z_ref[t][:, hs])).astype(BF16)

        kd = jnp.concatenate([k[t] * jnp.exp(gch[nt - 1] - gch[t]) for t in range(nt)],
                             axis=0).astype(BF16)
        uall = jnp.concatenate(u, axis=0)
        for s in range(SB):
            um = jnp.where(rows4 == s, uall, 0.0).astype(BF16)
            gl = jnp.exp(gch[nt - 1][s:s + 1, :])
            so_ref[s, h] = gl * s0_ref[s, h] + _dot_tn(kd, um)


def _delta_sample(ps, conv_state_t, delta_state, new_state, layer, cw, alog, dtb, og):
    ns = CONV_B_W - 1
    n_before = 11
    return pl.pallas_call(
        _delta_sample_body,
        grid=(DEC_BATCH // SB,),
        input_output_aliases={n_before: 2},
        in_specs=[
            pl.BlockSpec((DEC_SEQ, SB, COL_BLK), lambda s: (0, s, 1)),
            pl.BlockSpec((DEC_SEQ, SB, COL_BLK), lambda s: (0, s, 2)),
            pl.BlockSpec((DEC_SEQ, SB, COL_BLK), lambda s: (0, s, 3)),
            pl.BlockSpec((DEC_SEQ, SB, COL_BLK), lambda s: (0, s, 4)),
            pl.BlockSpec((DEC_SEQ, SB, SMALL_BLK), lambda s: (0, s, DB_COL_BLK)),
            pl.BlockSpec((ns, SB, 3 * D_B), lambda s: (0, s, 0)),
            pl.BlockSpec((SB, None, H_B, DK, DV), lambda s: (s, layer, 0, 0, 0)),
            pl.BlockSpec((CONV_B_W, 3 * D_B), lambda s: (0, 0)),
            pl.BlockSpec((1, LANES), lambda s: (0, 0)),
            pl.BlockSpec((1, LANES), lambda s: (0, 0)),
            pl.BlockSpec((1, DV), lambda s: (0, 0)),
            pl.BlockSpec(memory_space=pl.ANY),
        ],
        out_specs=[
            pl.BlockSpec((DEC_SEQ, SB, D_B), lambda s: (0, s, 0)),
            pl.BlockSpec((ns, SB, 3 * D_B), lambda s: (0, s, 0)),
            pl.BlockSpec((SB, None, H_B, DK, DV), lambda s: (s, layer, 0, 0, 0)),
        ],
        out_shape=[
            jax.ShapeDtypeStruct((DEC_SEQ, DEC_BATCH, D_B), BF16),
            jax.ShapeDtypeStruct((ns, DEC_BATCH, 3 * D_B), F32),
            jax.ShapeDtypeStruct((DEC_BATCH, DEPTH, H_B, DK, DV), F32),
        ],
        compiler_params=_cparams(("parallel",)),
        name="delta_sample",
    )(ps, ps, ps, ps, ps, conv_state_t, delta_state, cw, alog, dtb, og, new_state)


def _row(x):
    return x.reshape(1, -1)


def _pad_lanes(x):
    return jnp.pad(x, (0, LANES - x.shape[0])).reshape(1, LANES)


def kernel(x_prompt, x_sample, state_conv_a, state_conv_b, state_delta, norm_ffn1, ffn1_gate,
           ffn1_up, ffn1_down, norm_mix, w_in, conv_a_w, conv_a_b, norm_a_g, norm_a_b, conv_b_w,
           a_log, dt_bias, norm_o_g, norm_c_g, norm_c_b, sgu_w, sgu_b, w_out, norm_ffn2,
           ffn2_gate, ffn2_up, ffn2_down, norm_final):
    wg1, wu1, wd1 = ffn1_gate.astype(BF16), ffn1_up.astype(BF16), ffn1_down.astype(BF16)
    wg2, wu2, wd2 = ffn2_gate.astype(BF16), ffn2_up.astype(BF16), ffn2_down.astype(BF16)
    w_in_b = jnp.transpose(w_in, (0, 2, 1)).astype(BF16)
    w_uv_b = w_in_b[:, N_IN - 2 * D_C:, :]
    w_out_b = w_out.astype(BF16)

    x = [x_prompt.reshape(T_PROMPT, D_MODEL),
         jnp.transpose(x_sample, (1, 0, 2)).reshape(T_SAMPLE, D_MODEL)]
    gfin = _row(norm_final)
    s_s = jnp.zeros((DEC_BATCH, DEPTH, H_B, DK, DV), F32)

    wsm = jnp.repeat(sgu_w[:, :, :DEC_SEQ, :DEC_SEQ].reshape(DEPTH, H_C, DEC_SEQ * DEC_SEQ),
                     HEAD_C, axis=1)
    wsm = jnp.transpose(wsm, (0, 2, 1))
    bsm = jnp.transpose(jnp.repeat(sgu_b[:, :, :SUBLANES], HEAD_C, axis=1), (0, 2, 1))
    bst = jnp.transpose(sgu_b, (0, 2, 1))

    outs = {k: [] for k in ("ap", "bp", "sp", "vp", "as", "bs", "ss", "vs")}
    for i in range(DEPTH):
        x = _ffn(x, _row(norm_ffn1[i]), wg1, wu1, wd1, gfin,
                 layer=i, split_out=False, final_norm=False)[0]
        p = _proj_in(x, _row(norm_mix[i]), w_in_b, w_uv_b, i)
        ps = p[T_PROMPT:].reshape(DEC_SEQ, DEC_BATCH, N_PROJ)

        caw, cab = conv_a_w[i], _row(conv_a_b[i])
        nag, nab = _row(norm_a_g[i]), _row(norm_a_b[i])
        ya_p, a_p = _conv_a_prompt(p, caw, cab, nag, nab)
        ya_s, a_s = _conv_a_sample(ps, jnp.transpose(state_conv_a[:, i], (1, 0, 2)),
                                   caw, cab, nag, nab)
        a_s = jnp.transpose(a_s, (1, 0, 2))

        alog, dtb, og = _pad_lanes(a_log[i]), _pad_lanes(dt_bias[i]), _row(norm_o_g[i])
        yb_p, b_p, s_p = _delta_prompt(p, conv_b_w[i], alog, dtb, og)
        yb_s, b_s, s_s = _delta_sample(ps, jnp.transpose(state_conv_b[:, i], (1, 0, 2)),
                                       state_delta, s_s, i, conv_b_w[i], alog, dtb, og)
        b_s = jnp.transpose(b_s, (1, 0, 2))

        ncg, ncb = _row(norm_c_g[i]), _row(norm_c_b[i])
        yc_p, v_p = _gmlp_prompt(p, ncg, ncb, sgu_w[i], bst[i])
        yc_s, v_s = _gmlp_sample(ps, ncg, ncb, wsm[i], bsm[i])

        y_s = [y.reshape(T_SAMPLE, y.shape[-1]) for y in (ya_s, yb_s, yc_s)]
        x = _proj_out(x, [ya_p, yb_p, yc_p], y_s, w_out_b, i)
        last = i == DEPTH - 1
        x = _ffn([x], _row(norm_ffn2[i]), wg2, wu2, wd2, gfin,
                 layer=i, split_out=last, final_norm=last)

        outs["ap"].append(a_p); outs["bp"].append(b_p); outs["sp"].append(s_p)
        outs["vp"].append(v_p); outs["as"].append(a_s); outs["bs"].append(b_s)
        outs["vs"].append(jnp.transpose(v_s, (1, 0, 2)))

    y_prompt = x[0].reshape(BATCH, SEQ, D_MODEL)
    y_sample = jnp.transpose(x[1].reshape(DEC_SEQ, DEC_BATCH, D_MODEL), (1, 0, 2))
    st = lambda k: jnp.stack(outs[k], axis=1)
    return (y_prompt, y_sample, st("ap"), st("bp"), st("sp"), st("vp"),
            st("as"), st("bs"), s_s, st("vs"))
```
